```python
import math
import jax
import jax.numpy as jnp
from jax import lax
import numpy as np

D_MODEL = 1024
BATCH = 8
SEQ = 4096
DEPTH = 2

N_META = 16
HEAD_DIM = 128
HA = 4
HB = 4
DA = HA * HEAD_DIM
DB = HB * HEAD_DIM
CONV_WIDTH = 4
CHUNK = 64
Q_BLOCK = 128
W_IN_COLS = 4 * DA + 2 * HA + 3 * DB
SSM_WIDTH = D_MODEL
GROUP = 16
N_GROUPS = SSM_WIDTH // GROUP
STATE = 64
D_FF = int(math.ceil(8 * D_MODEL / 3 / 256)) * 256
N_EVEN = (DEPTH + 1) // 2
N_ODD = DEPTH // 2
EPS = 1e-6

kernel_name = "hybrid_deltanet_stickbreak_s5_trunk"


def rms_norm(x, w):
    xf = x.astype(jnp.float32)
    y = xf * lax.rsqrt(jnp.mean(xf * xf, axis=-1, keepdims=True) + EPS)
    return (y * w.astype(jnp.float32)).astype(x.dtype)


def gated_rms_norm(o, z, w):
    of = o.astype(jnp.float32)
    y = of * lax.rsqrt(jnp.mean(of * of, axis=-1, keepdims=True) + EPS) * w.astype(jnp.float32)
    return (y * jax.nn.silu(z.astype(jnp.float32))).astype(z.dtype)


def l2_normalize(t):
    return t * lax.rsqrt(jnp.sum(t * t, axis=-1, keepdims=True) + EPS)


def pad_seq(t, front, back):
    return jnp.pad(t, [(0, 0), (front, back)] + [(0, 0)] * (t.ndim - 2))


def causal_depthwise_conv(x, w):
    k_len = w.shape[0]
    return lax.conv_general_dilated(
        x, w[:, None, :], window_strides=(1,), padding=[(k_len - 1, 0)],
        dimension_numbers=("NWC", "WIO", "NWC"), feature_group_count=x.shape[-1])


def gated_delta_rule_chunked(q, k, v, g, beta):
    bsz, seq_len, n_h, dk = q.shape
    dv = v.shape[-1]
    n_chunks = seq_len // CHUNK

    def chunks(t):
        t = t.reshape((bsz, n_chunks, CHUNK, n_h) + t.shape[3:])
        return jnp.moveaxis(jnp.moveaxis(t, 1, 0), 3, 2)

    qc, kc, vc, gc, bc = chunks(q), chunks(k), chunks(v), chunks(g), chunks(beta)
    G = jnp.cumsum(gc, axis=-1)
    incl = jnp.tril(jnp.ones((CHUNK, CHUNK), dtype=bool))
    strict = jnp.tril(jnp.ones((CHUNK, CHUNK), dtype=bool), -1)
    diff = G[..., :, None] - G[..., None, :]
    decay = jnp.where(incl, jnp.exp(jnp.where(incl, diff, 0.0)), 0.0)
    kb = kc * bc[..., None]
    lower = jnp.where(strict, jnp.einsum("nbhid,nbhjd->nbhij", kb, kc) * decay, 0.0)
    eye = jnp.eye(CHUNK, dtype=lower.dtype)
    rhs = jnp.concatenate([vc * bc[..., None], kb * jnp.exp(G)[..., None]], axis=-1)
    sol = lax.linalg.triangular_solve(lower + eye, rhs, left_side=True, lower=True,
                                      unit_diagonal=True)
    u_c, w_c = sol[..., :dv], sol[..., dv:]
    qk = jnp.where(incl, jnp.einsum("nbhid,nbhjd->nbhij", qc, kc) * decay, 0.0)
    q_dec = qc * jnp.exp(G)[..., None]
    k_dec = kc * jnp.exp(G[..., -1:] - G)[..., None]
    g_last = jnp.exp(G[..., -1])

    def step(S, xs):
        qk_n, qd_n, kd_n, u_n, w_n, gl_n = xs
        v_new = u_n - jnp.einsum("bhck,bhkv->bhcv", w_n, S)
        o = jnp.einsum("bhck,bhkv->bhcv", qd_n, S) + jnp.einsum("bhij,bhjv->bhiv", qk_n, v_new)
        S = S * gl_n[..., None, None] + jnp.einsum("bhck,bhcv->bhkv", kd_n, v_new)
        return S, o

    S0 = jnp.zeros((bsz, n_h, dk, dv), dtype=qc.dtype)
    _, o = lax.scan(step, S0, (qk, q_dec, k_dec, u_c, w_c, g_last))
    o = jnp.moveaxis(jnp.moveaxis(o, 2, 3), 0, 1)
    return o.reshape(bsz, seq_len, n_h, dv)


def stick_breaking_attention(q, k, v):
    bsz, seq_len, n_h, dh = q.shape
    n_blocks = -(-seq_len // Q_BLOCK)
    lp = n_blocks * Q_BLOCK
    q, k, v = pad_seq(q, 0, lp - seq_len), pad_seq(k, 0, lp - seq_len), pad_seq(v, 0, lp - seq_len)
    scale = dh ** -0.5
    key_pos = jnp.arange(lp)

    def block(start):
        qb = lax.dynamic_slice_in_dim(q, start, Q_BLOCK, axis=1)
        z = jnp.einsum("bqhd,bkhd->bhqk", qb, k, preferred_element_type=jnp.float32) * scale
        causal = key_pos[None, :] < (start + jnp.arange(Q_BLOCK))[:, None]
        log_1m = jnp.where(causal, jax.nn.log_sigmoid(-z), 0.0)
        csum = jnp.cumsum(log_1m, axis=-1)
        log_w = jax.nn.log_sigmoid(z) + (csum[..., -1:] - csum)
        w = jnp.where(causal, jnp.exp(log_w), 0.0)
        return jnp.einsum("bhqk,bkhd->bqhd", w.astype(v.dtype), v)

    out = lax.map(block, jnp.arange(n_blocks) * Q_BLOCK)
    out = jnp.moveaxis(out, 0, 1).reshape(bsz, lp, n_h, dh)
    return out[:, :seq_len]


def deltanet_stickbreak_mixer(u, w_in, conv_w, a_log, dt_bias, out_norm, w_out):
    bsz, seq_len, _ = u.shape
    proj = u @ w_in
    qkv_a, z_a, b_a, a_a, qkv_b = jnp.split(
        proj, [3 * DA, 4 * DA, 4 * DA + HA, 4 * DA + 2 * HA], axis=-1)
    qkv_a = jax.nn.silu(causal_depthwise_conv(qkv_a, conv_w)).astype(jnp.float32)
    q_a, k_a, v_a = (t.reshape(bsz, seq_len, HA, HEAD_DIM) for t in jnp.split(qkv_a, 3, axis=-1))
    q_a = l2_normalize(q_a) * HEAD_DIM ** -0.5
    k_a = l2_normalize(k_a)
    beta = jax.nn.sigmoid(b_a.astype(jnp.float32))
    g = -jnp.exp(a_log.astype(jnp.float32)) * jax.nn.softplus(
        a_a.astype(jnp.float32) + dt_bias.astype(jnp.float32))
    front = CHUNK - N_META
    back = (-(seq_len + front)) % CHUNK
    o_a = gated_delta_rule_chunked(pad_seq(q_a, front, back), pad_seq(k_a, front, back),
                                   pad_seq(v_a, front, back), pad_seq(g, front, back),
                                   pad_seq(beta, front, back))[:, front:front + seq_len]
    o_a = gated_rms_norm(o_a, z_a.reshape(bsz, seq_len, HA, HEAD_DIM), out_norm)
    q_b, k_b, v_b = (t.reshape(bsz, seq_len, HB, HEAD_DIM) for t in jnp.split(qkv_b, 3, axis=-1))
    o_b = stick_breaking_attention(q_b, k_b, v_b)
    o = jnp.concatenate([o_a.reshape(bsz, seq_len, DA), o_b.reshape(bsz, seq_len, DB)], axis=-1)
    return o @ w_out


def s5_mixer(u, w_in, lam_re, lam_im, log_step, b_re, b_im, c_re, c_im, d_skip, w_glu):
    bsz, seq_len, _ = u.shape
    h = (u @ w_in).astype(jnp.float32)
    hg = h.reshape(bsz, seq_len, N_GROUPS, GROUP)
    lam_re = lam_re.astype(jnp.float32)
    lam_im = lam_im.astype(jnp.float32)
    dt = jnp.exp(log_step.astype(jnp.float32))[:, None]
    mag = jnp.exp(lam_re * dt)
    a_re = mag * jnp.cos(lam_im * dt)
    a_im = mag * jnp.sin(lam_im * dt)
    den = lam_re * lam_re + lam_im * lam_im
    f_re = ((a_re - 1.0) * lam_re + a_im * lam_im) / den
    f_im = (a_im * lam_re - (a_re - 1.0) * lam_im) / den
    b_re = b_re.astype(jnp.float32)
    b_im = b_im.astype(jnp.float32)
    bb_re = f_re[..., None] * b_re - f_im[..., None] * b_im
    bb_im = f_re[..., None] * b_im + f_im[..., None] * b_re
    bu_re = jnp.einsum("gpc,blgc->lbgp", bb_re, hg)
    bu_im = jnp.einsum("gpc,blgc->lbgp", bb_im, hg)
    a_re_l = jnp.broadcast_to(a_re, (seq_len, 1, N_GROUPS, STATE))
    a_im_l = jnp.broadcast_to(a_im, (seq_len, 1, N_GROUPS, STATE))

    def combine(e1, e2):
        ar1, ai1, br1, bi1 = e1
        ar2, ai2, br2, bi2 = e2
        return (ar1 * ar2 - ai1 * ai2, ar1 * ai2 + ai1 * ar2,
                ar2 * br1 - ai2 * bi1 + br2, ar2 * bi1 + ai2 * br1 + bi2)

    _, _, x_re, x_im = lax.associative_scan(combine, (a_re_l, a_im_l, bu_re, bu_im), axis=0)
    y = (jnp.einsum("gcp,lbgp->blgc", c_re.astype(jnp.float32), x_re)
         - jnp.einsum("gcp,lbgp->blgc", c_im.astype(jnp.float32), x_im))
    y = y.reshape(bsz, seq_len, SSM_WIDTH) + d_skip.astype(jnp.float32) * h
    y = jax.nn.gelu(y).astype(u.dtype)
    val, gate = jnp.split(y @ w_glu, 2, axis=-1)
    return val * jax.nn.sigmoid(gate)


def swiglu_ffn(x, w_gate, w_up, w_down):
    return (jax.nn.silu(x @ w_gate) * (x @ w_up)) @ w_down


def setup_inputs(seed: int = 0) -> dict:
    key = jax.random.key(seed)
    ks = jax.random.split(key, 32)
    f32 = jnp.float32

    def nrm(k, shape, scale):
        return jax.random.normal(k, shape, f32) * scale

    def gain(k, shape):
        return 1.0 + 0.02 * jax.random.normal(k, shape, f32)

    dt = jnp.exp(jax.random.uniform(ks[13], (N_EVEN, HA), f32, math.log(1e-3), math.log(1e-1)))
    lam_im0 = jnp.pi * jnp.arange(STATE, dtype=f32)
    return {
        "x": nrm(ks[0], (BATCH, SEQ, D_MODEL), 1.0),
        "meta_tokens": nrm(ks[1], (N_META, D_MODEL), 1.0),
        "norm_mix_pre": gain(ks[2], (DEPTH, D_MODEL)),
        "norm_mix_post": gain(ks[3], (DEPTH, D_MODEL)),
        "norm_ffn_pre": gain(ks[4], (DEPTH, D_MODEL)),
        "norm_ffn_post": gain(ks[5], (DEPTH, D_MODEL)),
        "ffn_w_gate": nrm(ks[6], (DEPTH, D_MODEL, D_FF), D_MODEL ** -0.5),
        "ffn_w_up": nrm(ks[7], (DEPTH, D_MODEL, D_FF), D_MODEL ** -0.5),
        "ffn_w_down": nrm(ks[8], (DEPTH, D_FF, D_MODEL), D_FF ** -0.5),
        "hyb_w_in": nrm(ks[9], (N_EVEN, D_MODEL, W_IN_COLS), D_MODEL ** -0.5),
        "hyb_conv_w": nrm(ks[10], (N_EVEN, CONV_WIDTH, 3 * DA), CONV_WIDTH ** -0.5),
        "dn_a_log": jnp.log(jax.random.uniform(ks[11], (N_EVEN, HA), f32, 1.0, 16.0)),
        "dn_dt_bias": dt + jnp.log(-jnp.expm1(-dt)),
        "dn_out_norm": gain(ks[12], (N_EVEN, HEAD_DIM)),
        "hyb_w_out": nrm(ks[14], (N_EVEN, DA + DB, D_MODEL), (DA + DB) ** -0.5),
        "ssm_w_in": nrm(ks[15], (N_ODD, D_MODEL, SSM_WIDTH), D_MODEL ** -0.5),
        "ssm_lambda_re": -0.5 + 0.01 * jax.random.normal(ks[16], (N_ODD, N_GROUPS, STATE), f32),
        "ssm_lambda_im": lam_im0 + 0.01 * jax.random.normal(ks[17], (N_ODD, N_GROUPS, STATE), f32),
        "ssm_log_step": jax.random.uniform(ks[18], (N_ODD, N_GROUPS), f32, math.log(1e-3), math.log(1e-1)),
        "ssm_b_re": nrm(ks[19], (N_ODD, N_GROUPS, STATE, GROUP), (2 * GROUP) ** -0.5),
        "ssm_b_im": nrm(ks[20], (N_ODD, N_GROUPS, STATE, GROUP), (2 * GROUP) ** -0.5),
        "ssm_c_re": nrm(ks[21], (N_ODD, N_GROUPS, GROUP, STATE), (2 * STATE) ** -0.5),
        "ssm_c_im": nrm(ks[22], (N_ODD, N_GROUPS, GROUP, STATE), (2 * STATE) ** -0.5),
        "ssm_d": nrm(ks[23], (N_ODD, SSM_WIDTH), 1.0),
        "ssm_w_glu": nrm(ks[24], (N_ODD, SSM_WIDTH, 2 * D_MODEL), SSM_WIDTH ** -0.5),
    }


def reference(x, meta_tokens, norm_mix_pre, norm_mix_post, norm_ffn_pre, norm_ffn_post,
              ffn_w_gate, ffn_w_up, ffn_w_down, hyb_w_in, hyb_conv_w, dn_a_log, dn_dt_bias,
              dn_out_norm, hyb_w_out, ssm_w_in, ssm_lambda_re, ssm_lambda_im, ssm_log_step,
              ssm_b_re, ssm_b_im, ssm_c_re, ssm_c_im, ssm_d, ssm_w_glu):
    bsz = x.shape[0]
    meta = jnp.broadcast_to(meta_tokens[None].astype(x.dtype), (bsz, N_META, x.shape[-1]))
    h = jnp.concatenate([meta, x], axis=1)
    for i in range(DEPTH):
        j = i // 2
        hn = rms_norm(h, norm_mix_pre[i])
        if i % 2 == 0:
            m = deltanet_stickbreak_mixer(hn, hyb_w_in[j], hyb_conv_w[j], dn_a_log[j],
                                          dn_dt_bias[j], dn_out_norm[j], hyb_w_out[j])
        else:
            m = s5_mixer(hn, ssm_w_in[j], ssm_lambda_re[j], ssm_lambda_im[j], ssm_log_step[j],
                         ssm_b_re[j], ssm_b_im[j], ssm_c_re[j], ssm_c_im[j], ssm_d[j], ssm_w_glu[j])
        h = h + rms_norm(m.astype(h.dtype), norm_mix_post[i])
        hn = rms_norm(h, norm_ffn_pre[i])
        f = swiglu_ffn(hn, ffn_w_gate[i], ffn_w_up[i], ffn_w_down[i])
        h = h + rms_norm(f, norm_ffn_post[i])
    return h[:, N_META:]
```

```python
import functools

import jax
import jax.numpy as jnp
from jax import lax
from jax.experimental import pallas as pl
from jax.experimental.pallas import tpu as pltpu

N_META = 16
HEAD_DIM = 128
HA = 4
HB = 4
DA = HA * HEAD_DIM
DB = HB * HEAD_DIM
CONV_WIDTH = 4
GROUP = 16
STATE = 64
EPS = 1e-6

LANE = 128
SEQ_ALIGN = 128
DN_CHUNK = 64
ATT_BLOCK = 128
S5_TILE = 64
GROUPS_PER_LANE_TILE = LANE // GROUP
S5_HALF = GROUPS_PER_LANE_TILE * STATE
VMEM_LIMIT = 56 * 1024 * 1024

F32 = jnp.float32
BF16 = jnp.bfloat16
HIGHEST = lax.Precision.HIGHEST


def _dot(a, b):
    return jnp.dot(a, b, preferred_element_type=F32)


def _dot_nt(a, b):
    return lax.dot_general(a, b, (((1,), (1,)), ((), ())), preferred_element_type=F32)


def _dot_tn(a, b):
    return lax.dot_general(a, b, (((0,), (0,)), ((), ())), preferred_element_type=F32)


def _dot_f32(a, b):
    return jnp.dot(a, b, preferred_element_type=F32, precision=HIGHEST)


def _rms(x, w):
    return x * lax.rsqrt(jnp.mean(x * x, axis=-1, keepdims=True) + EPS) * w


def _sigmoid(x):
    return 1.0 / (1.0 + jnp.exp(-x))


def _softplus(x):
    return jnp.maximum(x, 0.0) + jnp.log(1.0 + jnp.exp(-jnp.abs(x)))


def _const_spec(shape):
    zeros = (0,) * len(shape)
    return pl.BlockSpec(shape, lambda *_: zeros, pipeline_mode=pl.Buffered(1))


def _params(*sem):
    return pltpu.CompilerParams(dimension_semantics=sem, vmem_limit_bytes=VMEM_LIMIT)


def _norm_proj_kernel(x_ref, nw_ref, w_ref, *out_refs, col_splits):
    y = _rms(x_ref[...], nw_ref[...]).astype(BF16)
    for out_ref, (start, width) in zip(out_refs, col_splits):
        out_ref[...] = _dot(y, w_ref[:, start:start + width]).astype(out_ref.dtype)


def _norm_proj(x, norm_w, w, col_splits, out_dtypes, tm=512):
    t, d = x.shape
    n = w.shape[1]
    return pl.pallas_call(
        functools.partial(_norm_proj_kernel, col_splits=col_splits),
        grid=(t // tm,),
        in_specs=[pl.BlockSpec((tm, d), lambda i: (i, 0)),
                  _const_spec((1, d)),
                  _const_spec((d, n))],
        out_specs=[pl.BlockSpec((tm, width), lambda i: (i, 0)) for _, width in col_splits],
        out_shape=[jax.ShapeDtypeStruct((t, width), dt)
                   for (_, width), dt in zip(col_splits, out_dtypes)],
        compiler_params=_params("parallel"),
        name="norm_proj",
    )(x, norm_w.reshape(1, d), w)


def _unit_lower_inverse(lower, eye):
    c = lower.shape[0]
    x = -lower
    p = eye + x
    power = 2
    while power < c:
        x = _dot_f32(x, x)
        p = p + _dot_f32(p, x)
        power *= 2
    return p


def _deltanet_kernel(qkv_ref, z_ref, ba_ref, convw_ref, gpar_ref, onorm_ref, o_ref,
                     prev_ref, state_ref):
    c = qkv_ref.shape[1]

    @pl.when(pl.program_id(1) == 0)
    def _():
        prev_ref[...] = jnp.zeros_like(prev_ref)
        state_ref[...] = jnp.zeros_like(state_ref)

    x = qkv_ref[0]
    prev = prev_ref[...]
    row = lax.broadcasted_iota(jnp.int32, x.shape, 0)
    acc = x * convw_ref[CONV_WIDTH - 1:CONV_WIDTH, :]
    for s in range(1, CONV_WIDTH):
        shifted = jnp.where(row < s, pltpu.roll(prev, s, 0), pltpu.roll(x, s, 0))
        acc = acc + shifted * convw_ref[CONV_WIDTH - 1 - s:CONV_WIDTH - s, :]
    prev_ref[...] = x
    xc = acc * _sigmoid(acc)

    ba = ba_ref[0]
    beta_all = _sigmoid(ba)
    g_all = -jnp.exp(gpar_ref[0:1, :]) * _softplus(ba + gpar_ref[1:2, :])
    ri = lax.broadcasted_iota(jnp.int32, (c, c), 0)
    ci = lax.broadcasted_iota(jnp.int32, (c, c), 1)
    incl = ci <= ri
    strict = ci < ri
    eye = jnp.where(ci == ri, 1.0, 0.0).astype(F32)
    g_cum = _dot_f32(jnp.where(incl, 1.0, 0.0).astype(F32), g_all)
    sel = jnp.where(lax.broadcasted_iota(jnp.int32, (8, LANE), 1)
                    == lax.broadcasted_iota(jnp.int32, (8, LANE), 0) + HA, 1.0, 0.0).astype(F32)
    g_rows = lax.dot_general(sel, g_cum, (((1,), (1,)), ((), ())),
                             preferred_element_type=F32, precision=HIGHEST)

    z = z_ref[0]
    onorm = onorm_ref[...]
    for h in range(HA):
        sl = slice(h * HEAD_DIM, (h + 1) * HEAD_DIM)
        qh = xc[:, sl]
        kh = xc[:, DA + h * HEAD_DIM:DA + (h + 1) * HEAD_DIM]
        vh = xc[:, 2 * DA + h * HEAD_DIM:2 * DA + (h + 1) * HEAD_DIM]
        qh = qh * lax.rsqrt(jnp.sum(qh * qh, axis=-1, keepdims=True) + EPS) * (HEAD_DIM ** -0.5)
        kh = kh * lax.rsqrt(jnp.sum(kh * kh, axis=-1, keepdims=True) + EPS)
        beta = beta_all[:, h:h + 1]
        g_col = g_cum[:, HA + h:HA + h + 1]
        g_row = g_rows[h:h + 1, :]
        g_last = g_cum[c - 1:c, HA + h:HA + h + 1]
        decay = jnp.where(incl, jnp.exp(jnp.where(incl, g_col - g_row, 0.0)), 0.0)
        exp_g = jnp.exp(g_col)

        kb = kh * beta
        k_bf = kh.astype(BF16)
        lower = jnp.where(strict, _dot_nt(kb.astype(BF16), k_bf) * decay, 0.0)
        t_inv = _unit_lower_inverse(lower, eye)
        rhs = jnp.concatenate([vh * beta, kb * exp_g], axis=1).astype(BF16)
        sol = _dot(t_inv.astype(BF16), rhs)
        u = sol[:, :HEAD_DIM]
        w = sol[:, HEAD_DIM:]
        qk = jnp.where(incl, _dot_nt(qh.astype(BF16), k_bf) * decay, 0.0)
        q_dec = (qh * exp_g).astype(BF16)
        k_dec = (kh * jnp.exp(g_last - g_col)).astype(BF16)

        state = state_ref[h]
        state_bf = state.astype(BF16)
        v_new = u - _dot(w.astype(BF16), state_bf)
        v_new_bf = v_new.astype(BF16)
        o = _dot(q_dec, state_bf) + _dot(qk.astype(BF16), v_new_bf)
        state_ref[h] = state * jnp.exp(g_last) + _dot_tn(k_dec, v_new_bf)

        zh = z[:, sl]
        y = _rms(o, onorm) * (zh * _sigmoid(zh))
        o_ref[0, :, sl] = y.astype(o_ref.dtype)


def _deltanet(qkv, z, ba, conv_w, gate_par, out_norm):
    b, lp, _ = qkv.shape
    c = DN_CHUNK
    return pl.pallas_call(
        _deltanet_kernel,
        grid=(b, lp // c),
        in_specs=[pl.BlockSpec((1, c, 3 * DA), lambda bi, i: (bi, i, 0)),
                  pl.BlockSpec((1, c, DA), lambda bi, i: (bi, i, 0)),
                  pl.BlockSpec((1, c, LANE), lambda bi, i: (bi, i, 0)),
                  _const_spec((CONV_WIDTH, 3 * DA)),
                  _const_spec((2, LANE)),
                  _const_spec((1, HEAD_DIM))],
        out_specs=pl.BlockSpec((1, c, DA), lambda bi, i: (bi, i, 0)),
        out_shape=jax.ShapeDtypeStruct((b, lp, DA), BF16),
        scratch_shapes=[pltpu.VMEM((c, 3 * DA), F32),
                        pltpu.VMEM((HA, HEAD_DIM, HEAD_DIM), F32)],
        compiler_params=_params("parallel", "arbitrary"),
        name="deltanet",
    )(qkv, z, ba, conv_w, gate_par, out_norm)


def _stickbreak_kernel(q_ref, k_ref, v_ref, o_ref):
    tq = q_ref.shape[1]
    tk = ATT_BLOCK
    i = pl.program_id(2)
    q = q_ref[0]
    scale = HEAD_DIM ** -0.5
    q_pos = i * tq + lax.broadcasted_iota(jnp.int32, (tq, tk), 0)
    k_off = lax.broadcasted_iota(jnp.int32, (tq, tk), 1)
    suffix = jnp.where(lax.broadcasted_iota(jnp.int32, (tk, tk), 0)
                       >= lax.broadcasted_iota(jnp.int32, (tk, tk), 1), 1.0, 0.0).astype(BF16)

    def body(step, carry):
        acc, later = carry
        j = i - step
        start = pl.multiple_of(j * tk, tk)
        kj = k_ref[0, pl.ds(start, tk), :]
        vj = v_ref[0, pl.ds(start, tk), :]
        zq = _dot_nt(q, kj) * scale
        log_beta = jnp.minimum(zq, 0.0) - jnp.log(1.0 + jnp.exp(-jnp.abs(zq)))
        causal = (start + k_off) < q_pos
        log_1m = jnp.where(causal, log_beta - zq, 0.0)
        hi = log_1m.astype(BF16)
        lo = (log_1m - hi.astype(F32)).astype(BF16)
        suf = _dot(hi, suffix) + _dot(lo, suffix)
        log_w = log_beta + (suf - log_1m) + later
        wgt = jnp.where(causal, jnp.exp(log_w), 0.0)
        acc = acc + _dot(wgt.astype(BF16), vj)
        return acc, later + suf[:, 0:1]

    acc, _ = lax.fori_loop(0, i + 1, body,
                           (jnp.zeros((tq, HEAD_DIM), F32), jnp.zeros((tq, 1), F32)))
    o_ref[0] = acc.astype(o_ref.dtype)


def _stickbreak(qkv):
    b, lp, _ = qkv.shape
    tq = ATT_BLOCK
    return pl.pallas_call(
        _stickbreak_kernel,
        grid=(b, HB, lp // tq),
        in_specs=[pl.BlockSpec((1, tq, HEAD_DIM), lambda bi, h, i: (bi, i, h)),
                  pl.BlockSpec((1, lp, HEAD_DIM), lambda bi, h, i: (bi, 0, HB + h)),
                  pl.BlockSpec((1, lp, HEAD_DIM), lambda bi, h, i: (bi, 0, 2 * HB + h))],
        out_specs=pl.BlockSpec((1, tq, HEAD_DIM), lambda bi, h, i: (bi, i, h)),
        out_shape=jax.ShapeDtypeStruct((b, lp, DB), BF16),
        compiler_params=_params("parallel", "parallel", "arbitrary"),
        name="stickbreak",
    )(qkv, qkv, qkv)


def _residual_ffn(m, h_ref, npost_ref, nfpre_ref, wg_ref, wu_ref, wd_ref, nfpost_ref, out_ref):
    h1 = h_ref[...] + _rms(m, npost_ref[...])
    hn = _rms(h1, nfpre_ref[...]).astype(BF16)
    g = _dot(hn, wg_ref[...])
    u = _dot(hn, wu_ref[...])
    a = (g * _sigmoid(g) * u).astype(BF16)
    f = _dot(a, wd_ref[...])
    out_ref[...] = h1 + _rms(f, nfpost_ref[...])


def _hyb_out_ffn_kernel(oa_ref, ob_ref, wo_ref, *rest):
    m = _dot(oa_ref[...], wo_ref[:DA, :]) + _dot(ob_ref[...], wo_ref[DA:, :])
    _residual_ffn(m, *rest)


def _glu_out_ffn_kernel(y_ref, wglu_ref, *rest):
    d = wglu_ref.shape[1] // 2
    yv = _dot(y_ref[...], wglu_ref[...])
    m = yv[:, :d] * _sigmoid(yv[:, d:])
    _residual_ffn(m, *rest)


def _mixer_out_ffn(kernel_fn, acts, w_mix, h, npost, nfpre, wg, wu, wd, nfpost, tm=256):
    t, d = h.shape
    dff = wg.shape[1]
    row = lambda width: pl.BlockSpec((tm, width), lambda i: (i, 0))
    return pl.pallas_call(
        kernel_fn,
        grid=(t // tm,),
        in_specs=[row(a.shape[1]) for a in acts] + [
            _const_spec(w_mix.shape), row(d), _const_spec((1, d)), _const_spec((1, d)),
            _const_spec((d, dff)), _const_spec((d, dff)), _const_spec((dff, d)), _const_spec((1, d))],
        out_specs=row(d),
        out_shape=jax.ShapeDtypeStruct((t, d), F32),
        compiler_params=_params("parallel"),
        name="mixer_out_ffn",
    )(*acts, w_mix, h, npost.reshape(1, d), nfpre.reshape(1, d), wg, wu, wd, nfpost.reshape(1, d))


def _s5_kernel(hs_ref, bmat_ref, cmat_ref, a_ref, d_ref, y_ref, state_ref, bu_ref):
    tl, bsz, width = hs_ref.shape
    rows = tl * bsz
    n_tiles = width // LANE

    @pl.when(pl.program_id(0) == 0)
    def _():
        state_ref[...] = jnp.zeros_like(state_ref)

    for k in range(n_tiles):
        lanes = slice(k * LANE, (k + 1) * LANE)
        u = hs_ref[:, :, lanes].reshape(rows, LANE)
        bu_ref[...] = _dot(u.astype(BF16), bmat_ref[k])
        a_re = jnp.broadcast_to(a_ref[k, 0:1, :], (bsz, S5_HALF))
        a_im = jnp.broadcast_to(a_ref[k, 1:2, :], (bsz, S5_HALF))

        def step(t, carry):
            x_re, x_im = carry
            r0 = pl.multiple_of(t * bsz, bsz)
            bu = bu_ref[pl.ds(r0, bsz), :]
            n_re = a_re * x_re - a_im * x_im + bu[:, :S5_HALF]
            n_im = a_re * x_im + a_im * x_re + bu[:, S5_HALF:]
            bu_ref[pl.ds(r0, bsz), :] = jnp.concatenate([n_re, n_im], axis=1)
            return n_re, n_im

        st = state_ref[k]
        x_re, x_im = lax.fori_loop(0, tl, step, (st[:, :S5_HALF], st[:, S5_HALF:]), unroll=4)
        state_ref[k] = jnp.concatenate([x_re, x_im], axis=1)

        y = _dot(bu_ref[...].astype(BF16), cmat_ref[k]) + d_ref[:, lanes] * u
        y_ref[:, :, lanes] = jax.nn.gelu(y).reshape(tl, bsz, LANE).astype(y_ref.dtype)


def _s5_scan(hs_t, bmat, cmat, a_bar, d_skip):
    lp, bsz, width = hs_t.shape
    tl = S5_TILE
    n_tiles = width // LANE
    return pl.pallas_call(
        _s5_kernel,
        grid=(lp // tl,),
        in_specs=[pl.BlockSpec((tl, bsz, width), lambda i: (i, 0, 0)),
                  _const_spec(bmat.shape), _const_spec(cmat.shape),
                  _const_spec(a_bar.shape), _const_spec((1, width))],
        out_specs=pl.BlockSpec((tl, bsz, width), lambda i: (i, 0, 0)),
        out_shape=jax.ShapeDtypeStruct((lp, bsz, width), BF16),
        scratch_shapes=[pltpu.VMEM((n_tiles, bsz, 2 * S5_HALF), F32),
                        pltpu.VMEM((tl * bsz, 2 * S5_HALF), F32)],
        compiler_params=_params("arbitrary"),
        name="s5_scan",
    )(hs_t, bmat, cmat, a_bar, d_skip.reshape(1, width))


def _s5_discretise(lam_re, lam_im, log_step, b_re, b_im, c_re, c_im):
    n_groups = lam_re.shape[0]
    n_tiles = n_groups // GROUPS_PER_LANE_TILE
    dt = jnp.exp(log_step)[:, None]
    mag = jnp.exp(lam_re * dt)
    a_re = mag * jnp.cos(lam_im * dt)
    a_im = mag * jnp.sin(lam_im * dt)
    den = lam_re * lam_re + lam_im * lam_im
    f_re = ((a_re - 1.0) * lam_re + a_im * lam_im) / den
    f_im = (a_im * lam_re - (a_re - 1.0) * lam_im) / den
    bb_re = f_re[..., None] * b_re - f_im[..., None] * b_im
    bb_im = f_re[..., None] * b_im + f_im[..., None] * b_re
    eye = jnp.eye(GROUPS_PER_LANE_TILE, dtype=F32)

    def in_map(bb):
        t = bb.reshape(n_tiles, GROUPS_PER_LANE_TILE, STATE, GROUP)
        t = jnp.einsum("kgpc,gh->kgchp", t, eye)
        return t.reshape(n_tiles, LANE, S5_HALF)

    def out_map(cc):
        t = cc.reshape(n_tiles, GROUPS_PER_LANE_TILE, GROUP, STATE)
        t = jnp.einsum("kgcp,gh->kgphc", t, eye)
        return t.reshape(n_tiles, S5_HALF, LANE)

    bmat = jnp.concatenate([in_map(bb_re), in_map(bb_im)], axis=2).astype(BF16)
    cmat = jnp.concatenate([out_map(c_re), -out_map(c_im)], axis=1).astype(BF16)
    a_bar = jnp.stack([a_re.reshape(n_tiles, S5_HALF), a_im.reshape(n_tiles, S5_HALF)], axis=1)
    return bmat, cmat, a_bar


def kernel(x, meta_tokens, norm_mix_pre, norm_mix_post, norm_ffn_pre, norm_ffn_post, ffn_w_gate,
           ffn_w_up, ffn_w_down, hyb_w_in, hyb_conv_w, dn_a_log, dn_dt_bias, dn_out_norm, hyb_w_out,
           ssm_w_in, ssm_lambda_re, ssm_lambda_im, ssm_log_step, ssm_b_re, ssm_b_im, ssm_c_re,
           ssm_c_im, ssm_d, ssm_w_glu):
    bsz, seq, d = x.shape
    l_real = N_META + seq
    lp = -(-l_real // SEQ_ALIGN) * SEQ_ALIGN
    t = bsz * lp
    meta = jnp.broadcast_to(meta_tokens[None].astype(x.dtype), (bsz, N_META, d))
    h = jnp.concatenate([meta, x, jnp.zeros((bsz, lp - l_real, d), x.dtype)], axis=1).reshape(t, d)

    w_in = hyb_w_in[0]
    n_gate = 2 * HA
    w_cat = jnp.concatenate([
        w_in[:, :4 * DA],
        jnp.pad(w_in[:, 4 * DA:4 * DA + n_gate], ((0, 0), (0, LANE - n_gate))),
        w_in[:, 4 * DA + n_gate:]], axis=1).astype(BF16)
    splits = ((0, 3 * DA), (3 * DA, DA), (4 * DA, LANE), (4 * DA + LANE, 3 * DB))
    qkv_a, z_a, ba, qkv_b = _norm_proj(h, norm_mix_pre[0], w_cat, splits, (F32, F32, F32, BF16))
    gate_par = jnp.zeros((2, LANE), F32)
    gate_par = gate_par.at[0, HA:2 * HA].set(dn_a_log[0].astype(F32))
    gate_par = gate_par.at[1, HA:2 * HA].set(dn_dt_bias[0].astype(F32))
    o_a = _deltanet(qkv_a.reshape(bsz, lp, 3 * DA), z_a.reshape(bsz, lp, DA),
                    ba.reshape(bsz, lp, LANE), hyb_conv_w[0].astype(F32), gate_par,
                    dn_out_norm[0].reshape(1, HEAD_DIM).astype(F32))
    o_b = _stickbreak(qkv_b.reshape(bsz, lp, 3 * DB))
    h = _mixer_out_ffn(_hyb_out_ffn_kernel, [o_a.reshape(t, DA), o_b.reshape(t, DB)],
                       hyb_w_out[0].astype(BF16), h, norm_mix_post[0], norm_ffn_pre[0],
                       ffn_w_gate[0].astype(BF16), ffn_w_up[0].astype(BF16),
                       ffn_w_down[0].astype(BF16), norm_ffn_post[0])

    width = ssm_w_in.shape[2]
    (hs,) = _norm_proj(h, norm_mix_pre[1], ssm_w_in[0].astype(BF16), ((0, width),), (F32,))
    bmat, cmat, a_bar = _s5_discretise(ssm_lambda_re[0].astype(F32), ssm_lambda_im[0].astype(F32),
                                       ssm_log_step[0].astype(F32), ssm_b_re[0].astype(F32),
                                       ssm_b_im[0].astype(F32), ssm_c_re[0].astype(F32),
                                       ssm_c_im[0].astype(F32))
    hs_t = jnp.transpose(hs.reshape(bsz, lp, width), (1, 0, 2))
    y_t = _s5_scan(hs_t, bmat, cmat, a_bar, ssm_d[0].astype(F32))
    y = jnp.transpose(y_t, (1, 0, 2)).reshape(t, width)
    h = _mixer_out_ffn(_glu_out_ffn_kernel, [y], ssm_w_glu[0].astype(BF16), h, norm_mix_post[1],
                       norm_ffn_pre[1], ffn_w_gate[1].astype(BF16), ffn_w_up[1].astype(BF16),
                       ffn_w_down[1].astype(BF16), norm_ffn_post[1])
    return h.reshape(bsz, lp, d)[:, N_META:l_real]
```

```python
import functools

import jax
import jax.numpy as jnp
from jax import lax
from jax.experimental import pallas as pl
from jax.experimental.pallas import tpu as pltpu

N_META = 16
HEAD_DIM = 128
HA = 4
HB = 4
DA = HA * HEAD_DIM
DB = HB * HEAD_DIM
CONV_WIDTH = 4
GROUP = 16
STATE = 64
EPS = 1e-6

LANE = 128
SEQ_ALIGN = 128
DN_CHUNK = 128
INV_BASE = 16
ATT_BLOCK = 128
S5_TILE = 64
GROUPS_PER_LANE_TILE = LANE // GROUP
S5_HALF = GROUPS_PER_LANE_TILE * STATE
VMEM_LIMIT = 56 * 1024 * 1024
EXP_UNDERFLOW = -105.0

F32 = jnp.float32
BF16 = jnp.bfloat16
HIGHEST = lax.Precision.HIGHEST


def _dot(a, b):
    return jnp.dot(a, b, preferred_element_type=F32)


def _dot_nt(a, b):
    return lax.dot_general(a, b, (((1,), (1,)), ((), ())), preferred_element_type=F32)


def _dot_tn(a, b):
    return lax.dot_general(a, b, (((0,), (0,)), ((), ())), preferred_element_type=F32)


def _dot_f32(a, b):
    return jnp.dot(a, b, preferred_element_type=F32, precision=HIGHEST)


def _rms(x, w):
    return x * lax.rsqrt(jnp.mean(x * x, axis=-1, keepdims=True) + EPS) * w


def _sigmoid(x):
    return 1.0 / (1.0 + jnp.exp(-x))


def _softplus(x):
    return jnp.maximum(x, 0.0) + jnp.log(1.0 + jnp.exp(-jnp.abs(x)))


def _const_spec(shape):
    zeros = (0,) * len(shape)
    return pl.BlockSpec(shape, lambda *_: zeros, pipeline_mode=pl.Buffered(1))


def _params(*sem):
    return pltpu.CompilerParams(dimension_semantics=sem, vmem_limit_bytes=VMEM_LIMIT)


def _norm_proj_kernel(x_ref, nw_ref, w_ref, *out_refs, col_splits):
    y = _rms(x_ref[...], nw_ref[...]).astype(BF16)
    for out_ref, (start, width) in zip(out_refs, col_splits):
        out_ref[...] = _dot(y, w_ref[:, start:start + width]).astype(out_ref.dtype)


def _norm_proj(x, norm_w, w, col_splits, out_dtypes, tm=512):
    t, d = x.shape
    n = w.shape[1]
    return pl.pallas_call(
        functools.partial(_norm_proj_kernel, col_splits=col_splits),
        grid=(t // tm,),
        in_specs=[pl.BlockSpec((tm, d), lambda i: (i, 0)),
                  _const_spec((1, d)),
                  _const_spec((d, n))],
        out_specs=[pl.BlockSpec((tm, width), lambda i: (i, 0)) for _, width in col_splits],
        out_shape=[jax.ShapeDtypeStruct((t, width), dt)
                   for (_, width), dt in zip(col_splits, out_dtypes)],
        compiler_params=_params("parallel"),
        name="norm_proj",
    )(x, norm_w.reshape(1, d), w)


def _unit_lower_inverses(lowers, ri, ci):
    c = lowers[0].shape[0]

    def same_block(size):
        shift = size.bit_length() - 1
        return (ri >> shift) == (ci >> shift)

    def mm(a, b):
        return _dot(a.astype(BF16), b.astype(BF16))

    eye = jnp.where(ci == ri, 1.0, 0.0).astype(F32)
    in_base = same_block(INV_BASE)
    xs = [jnp.where(in_base, -low, 0.0) for low in lowers]
    ps = [eye + x for x in xs]
    power = 2
    while power < INV_BASE:
        xs = [mm(x, x) for x in xs]
        ps = [p + mm(p, x) for p, x in zip(ps, xs)]
        power *= 2
    size = INV_BASE
    while size < c:
        joins = jnp.logical_and(same_block(2 * size), jnp.logical_not(same_block(size)))
        ys = [mm(jnp.where(joins, low, 0.0), p) for low, p in zip(lowers, ps)]
        ps = [p - mm(p, y) for p, y in zip(ps, ys)]
        size *= 2
    return ps


def _deltanet_kernel(qkv_ref, z_ref, ba_ref, convw_ref, gpar_ref, onorm_ref, o_ref,
                     prev_ref, state_ref):
    c = qkv_ref.shape[1]
    heads = range(HA)

    @pl.when(pl.program_id(1) == 0)
    def _():
        prev_ref[...] = jnp.zeros_like(prev_ref)
        state_ref[...] = jnp.zeros_like(state_ref)

    x = qkv_ref[0]
    prev = prev_ref[...]
    row = lax.broadcasted_iota(jnp.int32, x.shape, 0)
    acc = x * convw_ref[CONV_WIDTH - 1:CONV_WIDTH, :]
    for s in range(1, CONV_WIDTH):
        shifted = jnp.where(row < s, pltpu.roll(prev, s, 0), pltpu.roll(x, s, 0))
        acc = acc + shifted * convw_ref[CONV_WIDTH - 1 - s:CONV_WIDTH - s, :]
    prev_ref[...] = x
    xc = acc * _sigmoid(acc)

    ba = ba_ref[0]
    beta_all = _sigmoid(ba)
    g_all = -jnp.exp(gpar_ref[0:1, :]) * _softplus(ba + gpar_ref[1:2, :])
    ri = lax.broadcasted_iota(jnp.int32, (c, c), 0)
    ci = lax.broadcasted_iota(jnp.int32, (c, c), 1)
    incl = ci <= ri
    strict = ci < ri
    g_cum = _dot_f32(jnp.where(incl, 1.0, 0.0).astype(F32), g_all)
    sel = jnp.where(lax.broadcasted_iota(jnp.int32, (8, LANE), 1)
                    == lax.broadcasted_iota(jnp.int32, (8, LANE), 0) + HA, 1.0, 0.0).astype(F32)
    g_rows = lax.dot_general(sel, g_cum, (((1,), (1,)), ((), ())),
                             preferred_element_type=F32, precision=HIGHEST)

    def unit(t):
        return t * lax.rsqrt(jnp.sum(t * t, axis=-1, keepdims=True) + EPS)

    def head_cols(base, h):
        return xc[:, base + h * HEAD_DIM:base + (h + 1) * HEAD_DIM]

    q = [unit(head_cols(0, h)) * (HEAD_DIM ** -0.5) for h in heads]
    k = [unit(head_cols(DA, h)) for h in heads]
    v = [head_cols(2 * DA, h) for h in heads]
    beta = [beta_all[:, h:h + 1] for h in heads]
    g_col = [g_cum[:, HA + h:HA + h + 1] for h in heads]
    g_last = [g_cum[c - 1:c, HA + h:HA + h + 1] for h in heads]
    decay = [jnp.where(incl, jnp.exp(jnp.where(incl, g_col[h] - g_rows[h:h + 1, :], 0.0)), 0.0)
             for h in heads]
    exp_g = [jnp.exp(g) for g in g_col]
    kb = [k[h] * beta[h] for h in heads]
    k_bf = [t.astype(BF16) for t in k]
    lower = [jnp.where(strict, _dot_nt(kb[h].astype(BF16), k_bf[h]) * decay[h], 0.0) for h in heads]
    qk = [jnp.where(incl, _dot_nt(q[h].astype(BF16), k_bf[h]) * decay[h], 0.0).astype(BF16)
          for h in heads]
    t_inv = _unit_lower_inverses(lower, ri, ci)
    rhs = [jnp.concatenate([v[h] * beta[h], kb[h] * exp_g[h]], axis=1).astype(BF16) for h in heads]
    sol = [_dot(t_inv[h].astype(BF16), rhs[h]) for h in heads]
    q_dec = [(q[h] * exp_g[h]).astype(BF16) for h in heads]
    k_dec = [(k[h] * jnp.exp(g_last[h] - g_col[h])).astype(BF16) for h in heads]

    state = [state_ref[h] for h in heads]
    state_bf = [s.astype(BF16) for s in state]
    v_new = [(sol[h][:, :HEAD_DIM] - _dot(sol[h][:, HEAD_DIM:].astype(BF16), state_bf[h])).astype(BF16)
             for h in heads]
    o = [_dot(q_dec[h], state_bf[h]) + _dot(qk[h], v_new[h]) for h in heads]
    for h in heads:
        state_ref[h] = state[h] * jnp.exp(g_last[h]) + _dot_tn(k_dec[h], v_new[h])

    z = z_ref[0]
    onorm = onorm_ref[...]
    for h in heads:
        sl = slice(h * HEAD_DIM, (h + 1) * HEAD_DIM)
        zh = z[:, sl]
        o_ref[0, :, sl] = (_rms(o[h], onorm) * (zh * _sigmoid(zh))).astype(o_ref.dtype)


def _deltanet(qkv, z, ba, conv_w, gate_par, out_norm):
    b, lp, _ = qkv.shape
    c = DN_CHUNK
    return pl.pallas_call(
        _deltanet_kernel,
        grid=(b, lp // c),
        in_specs=[pl.BlockSpec((1, c, 3 * DA), lambda bi, i: (bi, i, 0)),
                  pl.BlockSpec((1, c, DA), lambda bi, i: (bi, i, 0)),
                  pl.BlockSpec((1, c, LANE), lambda bi, i: (bi, i, 0)),
                  _const_spec((CONV_WIDTH, 3 * DA)),
                  _const_spec((2, LANE)),
                  _const_spec((1, HEAD_DIM))],
        out_specs=pl.BlockSpec((1, c, DA), lambda bi, i: (bi, i, 0)),
        out_shape=jax.ShapeDtypeStruct((b, lp, DA), BF16),
        scratch_shapes=[pltpu.VMEM((c, 3 * DA), F32),
                        pltpu.VMEM((HA, HEAD_DIM, HEAD_DIM), F32)],
        compiler_params=_params("parallel", "arbitrary"),
        name="deltanet",
    )(qkv, z, ba, conv_w, gate_par, out_norm)


def _stickbreak_blocks(qs, kjs, vjs, suffix, laters, mask):
    n = range(len(qs))
    zq = [_dot_nt(qs[h], kjs[h]) * (HEAD_DIM ** -0.5) for h in n]
    log_beta = [jnp.minimum(z, 0.0) - jnp.log(1.0 + jnp.exp(-jnp.abs(z))) for z in zq]
    log_1m = [log_beta[h] - zq[h] for h in n]
    if mask is not None:
        log_1m = [jnp.where(mask, t, 0.0) for t in log_1m]
    hi = [t.astype(BF16) for t in log_1m]
    lo = [(log_1m[h] - hi[h].astype(F32)).astype(BF16) for h in n]
    suf = [_dot(hi[h], suffix) + _dot(lo[h], suffix) for h in n]
    wgt = [jnp.exp(log_beta[h] + (suf[h] - log_1m[h]) + laters[h]) for h in n]
    if mask is not None:
        wgt = [jnp.where(mask, t, 0.0) for t in wgt]
    contrib = tuple(_dot(wgt[h].astype(BF16), vjs[h]) for h in n)
    return contrib, tuple(laters[h] + suf[h][:, 0:1] for h in n)


def _stickbreak_kernel(q_ref, k_ref, v_ref, o_ref):
    tq = q_ref.shape[1]
    tk = tq
    i = pl.program_id(1)
    heads = [slice(h * HEAD_DIM, (h + 1) * HEAD_DIM) for h in range(HB)]
    suffix = jnp.where(lax.broadcasted_iota(jnp.int32, (tk, tk), 0)
                       >= lax.broadcasted_iota(jnp.int32, (tk, tk), 1), 1.0, 0.0).astype(BF16)

    def visit(j, laters, mask):
        start = pl.multiple_of(j * tk, tk)
        contrib, new_laters = _stickbreak_blocks(
            [q_ref[0, :, sl] for sl in heads],
            [k_ref[0, pl.ds(start, tk), sl] for sl in heads],
            [v_ref[0, pl.ds(start, tk), sl] for sl in heads], suffix, laters, mask)
        largest = jnp.max(functools.reduce(jnp.maximum, new_laters))
        return contrib, new_laters, largest

    strictly_causal = (lax.broadcasted_iota(jnp.int32, (tq, tk), 1)
                       < lax.broadcasted_iota(jnp.int32, (tq, tk), 0))
    accs, laters, largest = visit(i, (jnp.zeros((tq, 1), F32),) * HB, strictly_causal)

    def more(carry):
        j, largest, _, _ = carry
        return jnp.logical_and(j >= 0, largest > EXP_UNDERFLOW)

    def body(carry):
        j, _, accs, laters = carry
        contrib, laters, largest = visit(j, laters, None)
        return j - 1, largest, tuple(a + c for a, c in zip(accs, contrib)), laters

    _, _, accs, _ = lax.while_loop(more, body, (i - 1, largest, accs, laters))
    for sl, acc in zip(heads, accs):
        o_ref[0, :, sl] = acc.astype(o_ref.dtype)


def _stickbreak(qkv):
    b, lp, _ = qkv.shape
    tq = ATT_BLOCK
    return pl.pallas_call(
        _stickbreak_kernel,
        grid=(b, lp // tq),
        in_specs=[pl.BlockSpec((1, tq, DB), lambda bi, i: (bi, i, 0)),
                  pl.BlockSpec((1, lp, DB), lambda bi, i: (bi, 0, 1)),
                  pl.BlockSpec((1, lp, DB), lambda bi, i: (bi, 0, 2))],
        out_specs=pl.BlockSpec((1, tq, DB), lambda bi, i: (bi, i, 0)),
        out_shape=jax.ShapeDtypeStruct((b, lp, DB), BF16),
        compiler_params=_params("parallel", "arbitrary"),
        name="stickbreak",
    )(qkv, qkv, qkv)


def _residual_ffn(m, h_ref, npost_ref, nfpre_ref, wg_ref, wu_ref, wd_ref, nfpost_ref, out_ref):
    h1 = h_ref[...] + _rms(m, npost_ref[...])
    hn = _rms(h1, nfpre_ref[...]).astype(BF16)
    g = _dot(hn, wg_ref[...])
    u = _dot(hn, wu_ref[...])
    a = (g * _sigmoid(g) * u).astype(BF16)
    f = _dot(a, wd_ref[...])
    out_ref[...] = h1 + _rms(f, nfpost_ref[...])


def _hyb_out_ffn_kernel(oa_ref, ob_ref, wo_ref, *rest):
    m = _dot(oa_ref[...], wo_ref[:DA, :]) + _dot(ob_ref[...], wo_ref[DA:, :])
    _residual_ffn(m, *rest)


def _glu_out_ffn_kernel(y_ref, wglu_ref, *rest):
    d = wglu_ref.shape[1] // 2
    yv = _dot(y_ref[...], wglu_ref[...])
    m = yv[:, :d] * _sigmoid(yv[:, d:])
    _residual_ffn(m, *rest)


def _mixer_out_ffn(kernel_fn, acts, w_mix, h, npost, nfpre, wg, wu, wd, nfpost, tm=256):
    t, d = h.shape
    dff = wg.shape[1]
    row = lambda width: pl.BlockSpec((tm, width), lambda i: (i, 0))
    return pl.pallas_call(
        kernel_fn,
        grid=(t // tm,),
        in_specs=[row(a.shape[1]) for a in acts] + [
            _const_spec(w_mix.shape), row(d), _const_spec((1, d)), _const_spec((1, d)),
            _const_spec((d, dff)), _const_spec((d, dff)), _const_spec((dff, d)), _const_spec((1, d))],
        out_specs=row(d),
        out_shape=jax.ShapeDtypeStruct((t, d), F32),
        compiler_params=_params("parallel"),
        name="mixer_out_ffn",
    )(*acts, w_mix, h, npost.reshape(1, d), nfpre.reshape(1, d), wg, wu, wd, nfpost.reshape(1, d))


def _s5_kernel(hs_ref, bmat_ref, cmat_ref, a_ref, d_ref, y_ref, state_ref, bu_ref):
    tl, bsz, width = hs_ref.shape
    rows = tl * bsz
    n_tiles = width // LANE

    @pl.when(pl.program_id(0) == 0)
    def _():
        state_ref[...] = jnp.zeros_like(state_ref)

    for k in range(n_tiles):
        lanes = slice(k * LANE, (k + 1) * LANE)
        u = hs_ref[:, :, lanes].reshape(rows, LANE)
        bu_ref[...] = _dot(u.astype(BF16), bmat_ref[k])
        a_re = jnp.broadcast_to(a_ref[k, 0:1, :], (bsz, S5_HALF))
        a_im = jnp.broadcast_to(a_ref[k, 1:2, :], (bsz, S5_HALF))

        def step(t, carry):
            x_re, x_im = carry
            r0 = pl.multiple_of(t * bsz, bsz)
            bu = bu_ref[pl.ds(r0, bsz), :]
            n_re = a_re * x_re - a_im * x_im + bu[:, :S5_HALF]
            n_im = a_re * x_im + a_im * x_re + bu[:, S5_HALF:]
            bu_ref[pl.ds(r0, bsz), :] = jnp.concatenate([n_re, n_im], axis=1)
            return n_re, n_im

        st = state_ref[k]
        x_re, x_im = lax.fori_loop(0, tl, step, (st[:, :S5_HALF], st[:, S5_HALF:]), unroll=4)
        state_ref[k] = jnp.concatenate([x_re, x_im], axis=1)

        y = _dot(bu_ref[...].astype(BF16), cmat_ref[k]) + d_ref[:, lanes] * u
        y_ref[:, :, lanes] = jax.nn.gelu(y).reshape(tl, bsz, LANE).astype(y_ref.dtype)


def _s5_scan(hs_t, bmat, cmat, a_bar, d_skip):
    lp, bsz, width = hs_t.shape
    tl = S5_TILE
    n_tiles = width // LANE
    return pl.pallas_call(
        _s5_kernel,
        grid=(lp // tl,),
        in_specs=[pl.BlockSpec((tl, bsz, width), lambda i: (i, 0, 0)),
                  _const_spec(bmat.shape), _const_spec(cmat.shape),
                  _const_spec(a_bar.shape), _const_spec((1, width))],
        out_specs=pl.BlockSpec((tl, bsz, width), lambda i: (i, 0, 0)),
        out_shape=jax.ShapeDtypeStruct((lp, bsz, width), BF16),
        scratch_shapes=[pltpu.VMEM((n_tiles, bsz, 2 * S5_HALF), F32),
                        pltpu.VMEM((tl * bsz, 2 * S5_HALF), F32)],
        compiler_params=_params("arbitrary"),
        name="s5_scan",
    )(hs_t, bmat, cmat, a_bar, d_skip.reshape(1, width))


def _s5_discretise(lam_re, lam_im, log_step, b_re, b_im, c_re, c_im):
    n_groups = lam_re.shape[0]
    n_tiles = n_groups // GROUPS_PER_LANE_TILE
    dt = jnp.exp(log_step)[:, None]
    mag = jnp.exp(lam_re * dt)
    a_re = mag * jnp.cos(lam_im * dt)
    a_im = mag * jnp.sin(lam_im * dt)
    den = lam_re * lam_re + lam_im * lam_im
    f_re = ((a_re - 1.0) * lam_re + a_im * lam_im) / den
    f_im = (a_im * lam_re - (a_re - 1.0) * lam_im) / den
    bb_re = f_re[..., None] * b_re - f_im[..., None] * b_im
    bb_im = f_re[..., None] * b_im + f_im[..., None] * b_re
    eye = jnp.eye(GROUPS_PER_LANE_TILE, dtype=F32)

    def in_map(bb):
        t = bb.reshape(n_tiles, GROUPS_PER_LANE_TILE, STATE, GROUP)
        t = jnp.einsum("kgpc,gh->kgchp", t, eye)
        return t.reshape(n_tiles, LANE, S5_HALF)

    def out_map(cc):
        t = cc.reshape(n_tiles, GROUPS_PER_LANE_TILE, GROUP, STATE)
        t = jnp.einsum("kgcp,gh->kgphc", t, eye)
        return t.reshape(n_tiles, S5_HALF, LANE)

    bmat = jnp.concatenate([in_map(bb_re), in_map(bb_im)], axis=2).astype(BF16)
    cmat = jnp.concatenate([out_map(c_re), -out_map(c_im)], axis=1).astype(BF16)
    a_bar = jnp.stack([a_re.reshape(n_tiles, S5_HALF), a_im.reshape(n_tiles, S5_HALF)], axis=1)
    return bmat, cmat, a_bar


def kernel(x, meta_tokens, norm_mix_pre, norm_mix_post, norm_ffn_pre, norm_ffn_post, ffn_w_gate,
           ffn_w_up, ffn_w_down, hyb_w_in, hyb_conv_w, dn_a_log, dn_dt_bias, dn_out_norm, hyb_w_out,
           ssm_w_in, ssm_lambda_re, ssm_lambda_im, ssm_log_step, ssm_b_re, ssm_b_im, ssm_c_re,
           ssm_c_im, ssm_d, ssm_w_glu):
    bsz, seq, d = x.shape
    l_real = N_META + seq
    lp = -(-l_real // SEQ_ALIGN) * SEQ_ALIGN
    t = bsz * lp
    meta = jnp.broadcast_to(meta_tokens[None].astype(x.dtype), (bsz, N_META, d))
    h = jnp.concatenate([meta, x, jnp.zeros((bsz, lp - l_real, d), x.dtype)], axis=1).reshape(t, d)

    w_in = hyb_w_in[0]
    n_gate = 2 * HA
    w_cat = jnp.concatenate([
        w_in[:, :4 * DA],
        jnp.pad(w_in[:, 4 * DA:4 * DA + n_gate], ((0, 0), (0, LANE - n_gate))),
        w_in[:, 4 * DA + n_gate:]], axis=1).astype(BF16)
    splits = ((0, 3 * DA), (3 * DA, DA), (4 * DA, LANE), (4 * DA + LANE, 3 * DB))
    qkv_a, z_a, ba, qkv_b = _norm_proj(h, norm_mix_pre[0], w_cat, splits, (F32, F32, F32, BF16))
    gate_par = jnp.zeros((2, LANE), F32)
    gate_par = gate_par.at[0, HA:2 * HA].set(dn_a_log[0].astype(F32))
    gate_par = gate_par.at[1, HA:2 * HA].set(dn_dt_bias[0].astype(F32))
    o_a = _deltanet(qkv_a.reshape(bsz, lp, 3 * DA), z_a.reshape(bsz, lp, DA),
                    ba.reshape(bsz, lp, LANE), hyb_conv_w[0].astype(F32), gate_par,
                    dn_out_norm[0].reshape(1, HEAD_DIM).astype(F32))
    o_b = _stickbreak(qkv_b.reshape(bsz, lp, 3 * DB))
    h = _mixer_out_ffn(_hyb_out_ffn_kernel, [o_a.reshape(t, DA), o_b.reshape(t, DB)],
                       hyb_w_out[0].astype(BF16), h, norm_mix_post[0], norm_ffn_pre[0],
                       ffn_w_gate[0].astype(BF16), ffn_w_up[0].astype(BF16),
                       ffn_w_down[0].astype(BF16), norm_ffn_post[0])

    width = ssm_w_in.shape[2]
    (hs,) = _norm_proj(h, norm_mix_pre[1], ssm_w_in[0].astype(BF16), ((0, width),), (F32,))
    bmat, cmat, a_bar = _s5_discretise(ssm_lambda_re[0].astype(F32), ssm_lambda_im[0].astype(F32),
                                       ssm_log_step[0].astype(F32), ssm_b_re[0].astype(F32),
                                       ssm_b_im[0].astype(F32), ssm_c_re[0].astype(F32),
                                       ssm_c_im[0].astype(F32))
    hs_t = jnp.transpose(hs.reshape(bsz, lp, width), (1, 0, 2))
    y_t = _s5_scan(hs_t, bmat, cmat, a_bar, ssm_d[0].astype(F32))
    y = jnp.transpose(y_t, (1, 0, 2)).reshape(t, width)
    h = _mixer_out_ffn(_glu_out_ffn_kernel, [y], ssm_w_glu[0].astype(BF16), h, norm_mix_post[1],
                       norm_ffn_pre[1], ffn_w_gate[1].astype(BF16), ffn_w_up[1].astype(BF16),
                       ffn_w_down[1].astype(BF16), norm_ffn_post[1])
    return h.reshape(bsz, lp, d)[:, N_META:l_real]
```

```python
import functools

import jax
import jax.numpy as jnp
from jax import lax
from jax.experimental import pallas as pl
from jax.experimental.pallas import tpu as pltpu

N_META = 16
HEAD_DIM = 128
HA = 4
HB = 4
DA = HA * HEAD_DIM
DB = HB * HEAD_DIM
CONV_WIDTH = 4
GROUP = 16
STATE = 64
EPS = 1e-6

LANE = 128
SEQ_ALIGN = 128
DN_CHUNK = 128
DN_SEQS_PER_STEP = 2
INV_BASE = 16
ATT_BLOCK = 128
S5_TILE = 64
GROUPS_PER_LANE_TILE = LANE // GROUP
S5_HALF = GROUPS_PER_LANE_TILE * STATE
VMEM_LIMIT = 56 * 1024 * 1024
EXP_UNDERFLOW = -105.0

F32 = jnp.float32
BF16 = jnp.bfloat16
HIGHEST = lax.Precision.HIGHEST


def _dot(a, b):
    return jnp.dot(a, b, preferred_element_type=F32)


def _dot_nt(a, b):
    return lax.dot_general(a, b, (((1,), (1,)), ((), ())), preferred_element_type=F32)


def _dot_tn(a, b):
    return lax.dot_general(a, b, (((0,), (0,)), ((), ())), preferred_element_type=F32)


def _dot_f32(a, b):
    return jnp.dot(a, b, preferred_element_type=F32, precision=HIGHEST)


def _rms(x, w):
    return x * lax.rsqrt(jnp.mean(x * x, axis=-1, keepdims=True) + EPS) * w


def _sigmoid(x):
    return 1.0 / (1.0 + jnp.exp(-x))


def _softplus(x):
    return jnp.maximum(x, 0.0) + jnp.log(1.0 + jnp.exp(-jnp.abs(x)))


def _const_spec(shape):
    zeros = (0,) * len(shape)
    return pl.BlockSpec(shape, lambda *_: zeros, pipeline_mode=pl.Buffered(1))


def _params(*sem):
    return pltpu.CompilerParams(dimension_semantics=sem, vmem_limit_bytes=VMEM_LIMIT)


def _norm_proj_kernel(x_ref, nw_ref, w_ref, *out_refs, col_splits):
    y = _rms(x_ref[...], nw_ref[...]).astype(BF16)
    for out_ref, (start, width) in zip(out_refs, col_splits):
        out_ref[...] = _dot(y, w_ref[:, start:start + width]).astype(out_ref.dtype)


def _norm_proj(x, norm_w, w, col_splits, out_dtypes, tm=512):
    t, d = x.shape
    n = w.shape[1]
    return pl.pallas_call(
        functools.partial(_norm_proj_kernel, col_splits=col_splits),
        grid=(t // tm,),
        in_specs=[pl.BlockSpec((tm, d), lambda i: (i, 0)),
                  _const_spec((1, d)),
                  _const_spec((d, n))],
        out_specs=[pl.BlockSpec((tm, width), lambda i: (i, 0)) for _, width in col_splits],
        out_shape=[jax.ShapeDtypeStruct((t, width), dt)
                   for (_, width), dt in zip(col_splits, out_dtypes)],
        compiler_params=_params("parallel"),
        name="norm_proj",
    )(x, norm_w.reshape(1, d), w)


HYB_COLS = (("qkv_a", 0, 3 * DA), ("z_a", 3 * DA, DA), ("gates", 4 * DA, LANE),
            ("qkv_b", 4 * DA + LANE, 3 * DB))
HALO = 8


def _hyb_in_kernel(x_ref, nw_ref, w_ref, convw_ref, gpar_ref, qkv_ref, zs_ref, gate_ref, qkvb_ref,
                   tail_ref):
    cols = {name: slice(start, start + width) for name, start, width in HYB_COLS}

    @pl.when(pl.program_id(1) == 0)
    def _():
        tail_ref[...] = jnp.zeros_like(tail_ref)

    y = _rms(x_ref[0], nw_ref[...]).astype(BF16)
    first_rows = lax.broadcasted_iota(jnp.int32, (HALO, DA), 0)

    def conv_silu(pre, cs):
        tail = tail_ref[:, cs]
        acc = pre * convw_ref[CONV_WIDTH - 1:CONV_WIDTH, cs]
        for s in range(1, CONV_WIDTH):
            rolled = pltpu.roll(pre, s, 0)
            head = jnp.where(first_rows < s, pltpu.roll(tail, s, 0), rolled[:HALO])
            shifted = jnp.concatenate([head, rolled[HALO:]], axis=0)
            acc = acc + shifted * convw_ref[CONV_WIDTH - 1 - s:CONV_WIDTH - s, cs]
        tail_ref[:, cs] = pre[pre.shape[0] - HALO:, :]
        return acc * _sigmoid(acc)

    def unit_heads(xc, base, gain):
        for h in range(HA):
            t = xc[:, h * HEAD_DIM:(h + 1) * HEAD_DIM]
            t = t * (lax.rsqrt(jnp.sum(t * t, axis=-1, keepdims=True) + EPS) * gain)
            qkv_ref[0, :, base + h * HEAD_DIM:base + (h + 1) * HEAD_DIM] = t.astype(qkv_ref.dtype)

    def post_q(pre):
        unit_heads(conv_silu(pre, slice(0, DA)), 0, HEAD_DIM ** -0.5)

    def post_k(pre):
        unit_heads(conv_silu(pre, slice(DA, 2 * DA)), DA, 1.0)

    def post_v(pre):
        qkv_ref[0, :, 2 * DA:] = conv_silu(pre, slice(2 * DA, 3 * DA)).astype(qkv_ref.dtype)

    def post_z(zlin):
        zs_ref[0] = (zlin * _sigmoid(zlin)).astype(zs_ref.dtype)

    def post_gates(ba):
        g = -jnp.exp(gpar_ref[0:1, :]) * _softplus(ba + gpar_ref[1:2, :])
        gate_ref[0] = jnp.where(lax.broadcasted_iota(jnp.int32, ba.shape, 1) < HA, _sigmoid(ba), g)

    def post_b(part):
        def store(pre):
            qkvb_ref[0, :, part * DB:(part + 1) * DB] = pre.astype(qkvb_ref.dtype)
        return store

    qa, qb = cols["qkv_a"].start, cols["qkv_b"].start
    stages = [(slice(qa, qa + DA), post_q), (slice(qa + DA, qa + 2 * DA), post_k),
              (slice(qa + 2 * DA, qa + 3 * DA), post_v), (cols["z_a"], post_z),
              (cols["gates"], post_gates)]
    stages += [(slice(qb + part * DB, qb + (part + 1) * DB), post_b(part)) for part in range(3)]
    pending = _dot(y, w_ref[:, stages[0][0]])
    for idx, (_, post) in enumerate(stages):
        upcoming = _dot(y, w_ref[:, stages[idx + 1][0]]) if idx + 1 < len(stages) else None
        post(pending)
        pending = upcoming


def _hyb_in(h, norm_w, w, conv_w, gate_par):
    b, lp, d = h.shape
    tm = next(rows for rows in (384, 256, 128) if lp % rows == 0)
    row = lambda width: pl.BlockSpec((1, tm, width), lambda bi, i: (bi, i, 0))
    widths = {name: width for name, _, width in HYB_COLS}
    return pl.pallas_call(
        _hyb_in_kernel,
        grid=(b, lp // tm),
        in_specs=[row(d), _const_spec((1, d)), _const_spec(w.shape),
                  _const_spec(conv_w.shape), _const_spec(gate_par.shape)],
        out_specs=[row(widths["qkv_a"]), row(widths["z_a"]), row(widths["gates"]), row(widths["qkv_b"])],
        out_shape=[jax.ShapeDtypeStruct((b, lp, widths["qkv_a"]), BF16),
                   jax.ShapeDtypeStruct((b, lp, widths["z_a"]), BF16),
                   jax.ShapeDtypeStruct((b, lp, widths["gates"]), F32),
                   jax.ShapeDtypeStruct((b, lp, widths["qkv_b"]), BF16)],
        scratch_shapes=[pltpu.VMEM((HALO, widths["qkv_a"]), F32)],
        compiler_params=_params("parallel", "arbitrary"),
        name="hyb_in",
    )(h, norm_w.reshape(1, d), w, conv_w, gate_par)


def _unit_lower_inverses(lowers, ri, ci):
    c = lowers[0].shape[0]

    def same_block(size):
        shift = size.bit_length() - 1
        return (ri >> shift) == (ci >> shift)

    def mm(a, b):
        return _dot(a.astype(BF16), b.astype(BF16))

    eye = jnp.where(ci == ri, 1.0, 0.0).astype(F32)
    in_base = same_block(INV_BASE)
    xs = [jnp.where(in_base, -low, 0.0) for low in lowers]
    ps = [eye + x for x in xs]
    power = 2
    while power < INV_BASE:
        xs = [mm(x, x) for x in xs]
        ps = [p + mm(p, x) for p, x in zip(ps, xs)]
        power *= 2
    size = INV_BASE
    while size < c:
        joins = jnp.logical_and(same_block(2 * size), jnp.logical_not(same_block(size)))
        ys = [mm(jnp.where(joins, low, 0.0), p) for low, p in zip(lowers, ps)]
        ps = [p - mm(p, y) for p, y in zip(ps, ys)]
        size *= 2
    return ps


def _deltanet_kernel(qkv_ref, zs_ref, gate_ref, onorm_ref, o_ref, state_ref):
    nseq, c = qkv_ref.shape[0], qkv_ref.shape[1]
    units = [(s, h) for s in range(nseq) for h in range(HA)]
    n = range(len(units))

    @pl.when(pl.program_id(1) == 0)
    def _():
        state_ref[...] = jnp.zeros_like(state_ref)

    ri = lax.broadcasted_iota(jnp.int32, (c, c), 0)
    ci = lax.broadcasted_iota(jnp.int32, (c, c), 1)
    incl = ci <= ri
    strict = ci < ri
    tril = jnp.where(incl, 1.0, 0.0).astype(F32)
    sel = jnp.where(lax.broadcasted_iota(jnp.int32, (8, LANE), 1)
                    == lax.broadcasted_iota(jnp.int32, (8, LANE), 0) + HA, 1.0, 0.0).astype(F32)
    gates = [gate_ref[s] for s in range(nseq)]
    g_cum = [_dot_f32(tril, g) for g in gates]
    g_rows = [lax.dot_general(sel, g, (((1,), (1,)), ((), ())), preferred_element_type=F32,
                              precision=HIGHEST) for g in g_cum]

    def head_cols(base, s, h):
        return qkv_ref[s, :, base + h * HEAD_DIM:base + (h + 1) * HEAD_DIM]

    q_bf = [head_cols(0, s, h) for s, h in units]
    k_bf = [head_cols(DA, s, h) for s, h in units]
    q = [t.astype(F32) for t in q_bf]
    k = [t.astype(F32) for t in k_bf]
    v = [head_cols(2 * DA, s, h).astype(F32) for s, h in units]
    beta = [gates[s][:, h:h + 1] for s, h in units]
    g_col = [g_cum[s][:, HA + h:HA + h + 1] for s, h in units]
    g_row = [g_rows[s][h:h + 1, :] for s, h in units]
    g_last = [g_cum[s][c - 1:c, HA + h:HA + h + 1] for s, h in units]
    decay = [jnp.where(incl, jnp.exp(jnp.where(incl, g_col[u] - g_row[u], 0.0)), 0.0) for u in n]
    exp_g = [jnp.exp(g) for g in g_col]
    kb = [k[u] * beta[u] for u in n]
    lower = [jnp.where(strict, _dot_nt(kb[u].astype(BF16), k_bf[u]) * decay[u], 0.0) for u in n]
    qk = [jnp.where(incl, _dot_nt(q_bf[u], k_bf[u]) * decay[u], 0.0).astype(BF16) for u in n]
    t_inv = _unit_lower_inverses(lower, ri, ci)
    rhs = [jnp.concatenate([v[u] * beta[u], kb[u] * exp_g[u]], axis=1).astype(BF16) for u in n]
    sol = [_dot(t_inv[u].astype(BF16), rhs[u]) for u in n]
    q_dec = [(q[u] * exp_g[u]).astype(BF16) for u in n]
    k_dec = [(k[u] * jnp.exp(g_last[u] - g_col[u])).astype(BF16) for u in n]

    state = [state_ref[u] for u in n]
    state_bf = [s.astype(BF16) for s in state]
    v_new = [(sol[u][:, :HEAD_DIM] - _dot(sol[u][:, HEAD_DIM:].astype(BF16), state_bf[u])).astype(BF16)
             for u in n]
    o = [_dot(q_dec[u], state_bf[u]) + _dot(qk[u], v_new[u]) for u in n]
    for u in n:
        state_ref[u] = state[u] * jnp.exp(g_last[u]) + _dot_tn(k_dec[u], v_new[u])

    onorm = onorm_ref[...]
    for u, (s, h) in enumerate(units):
        sl = slice(h * HEAD_DIM, (h + 1) * HEAD_DIM)
        o_ref[s, :, sl] = (_rms(o[u], onorm) * zs_ref[s, :, sl].astype(F32)).astype(o_ref.dtype)


def _deltanet(qkv, zs, gates, out_norm):
    b, lp, _ = qkv.shape
    c = DN_CHUNK
    nseq = DN_SEQS_PER_STEP
    return pl.pallas_call(
        _deltanet_kernel,
        grid=(b // nseq, lp // c),
        in_specs=[pl.BlockSpec((nseq, c, 3 * DA), lambda bi, i: (bi, i, 0)),
                  pl.BlockSpec((nseq, c, DA), lambda bi, i: (bi, i, 0)),
                  pl.BlockSpec((nseq, c, LANE), lambda bi, i: (bi, i, 0)),
                  _const_spec((1, HEAD_DIM))],
        out_specs=pl.BlockSpec((nseq, c, DA), lambda bi, i: (bi, i, 0)),
        out_shape=jax.ShapeDtypeStruct((b, lp, DA), BF16),
        scratch_shapes=[pltpu.VMEM((nseq * HA, HEAD_DIM, HEAD_DIM), F32)],
        compiler_params=_params("parallel", "arbitrary"),
        name="deltanet",
    )(qkv, zs, gates, out_norm)


def _stickbreak_blocks(qs, kjs, vjs, suffix, laters, mask):
    n = range(len(qs))
    zq = [_dot_nt(qs[h], kjs[h]) * (HEAD_DIM ** -0.5) for h in n]
    log_beta = [jnp.minimum(z, 0.0) - jnp.log(1.0 + jnp.exp(-jnp.abs(z))) for z in zq]
    log_1m = [log_beta[h] - zq[h] for h in n]
    if mask is not None:
        log_1m = [jnp.where(mask, t, 0.0) for t in log_1m]
    hi = [t.astype(BF16) for t in log_1m]
    lo = [(log_1m[h] - hi[h].astype(F32)).astype(BF16) for h in n]
    suf = [_dot(hi[h], suffix) + _dot(lo[h], suffix) for h in n]
    wgt = [jnp.exp(log_beta[h] + (suf[h] - log_1m[h]) + laters[h]) for h in n]
    if mask is not None:
        wgt = [jnp.where(mask, t, 0.0) for t in wgt]
    contrib = tuple(_dot(wgt[h].astype(BF16), vjs[h]) for h in n)
    return contrib, tuple(laters[h] + suf[h][:, 0:1] for h in n)


def _stickbreak_kernel(q_ref, k_ref, v_ref, o_ref):
    tq = q_ref.shape[1]
    tk = tq
    i = pl.program_id(1)
    heads = [slice(h * HEAD_DIM, (h + 1) * HEAD_DIM) for h in range(HB)]
    suffix = jnp.where(lax.broadcasted_iota(jnp.int32, (tk, tk), 0)
                       >= lax.broadcasted_iota(jnp.int32, (tk, tk), 1), 1.0, 0.0).astype(BF16)

    def visit(j, laters, mask):
        start = pl.multiple_of(j * tk, tk)
        contrib, new_laters = _stickbreak_blocks(
            [q_ref[0, :, sl] for sl in heads],
            [k_ref[0, pl.ds(start, tk), sl] for sl in heads],
            [v_ref[0, pl.ds(start, tk), sl] for sl in heads], suffix, laters, mask)
        largest = jnp.max(functools.reduce(jnp.maximum, new_laters))
        return contrib, new_laters, largest

    strictly_causal = (lax.broadcasted_iota(jnp.int32, (tq, tk), 1)
                       < lax.broadcasted_iota(jnp.int32, (tq, tk), 0))
    accs, laters, largest = visit(i, (jnp.zeros((tq, 1), F32),) * HB, strictly_causal)

    def more(carry):
        j, largest, _, _ = carry
        return jnp.logical_and(j >= 0, largest > EXP_UNDERFLOW)

    def body(carry):
        j, _, accs, laters = carry
        contrib, laters, largest = visit(j, laters, None)
        return j - 1, largest, tuple(a + c for a, c in zip(accs, contrib)), laters

    _, _, accs, _ = lax.while_loop(more, body, (i - 1, largest, accs, laters))
    for sl, acc in zip(heads, accs):
        o_ref[0, :, sl] = acc.astype(o_ref.dtype)


def _stickbreak(qkv):
    b, lp, _ = qkv.shape
    tq = ATT_BLOCK
    return pl.pallas_call(
        _stickbreak_kernel,
        grid=(b, lp // tq),
        in_specs=[pl.BlockSpec((1, tq, DB), lambda bi, i: (bi, i, 0)),
                  pl.BlockSpec((1, lp, DB), lambda bi, i: (bi, 0, 1)),
                  pl.BlockSpec((1, lp, DB), lambda bi, i: (bi, 0, 2))],
        out_specs=pl.BlockSpec((1, tq, DB), lambda bi, i: (bi, i, 0)),
        out_shape=jax.ShapeDtypeStruct((b, lp, DB), BF16),
        compiler_params=_params("parallel", "arbitrary"),
        name="stickbreak",
    )(qkv, qkv, qkv)


def _residual_ffn(m, h_ref, npost_ref, nfpre_ref, wg_ref, wu_ref, wd_ref, nfpost_ref, out_ref):
    h1 = h_ref[...] + _rms(m, npost_ref[...])
    hn = _rms(h1, nfpre_ref[...]).astype(BF16)
    g = _dot(hn, wg_ref[...])
    u = _dot(hn, wu_ref[...])
    a = (g * _sigmoid(g) * u).astype(BF16)
    f = _dot(a, wd_ref[...])
    out_ref[...] = h1 + _rms(f, nfpost_ref[...])


def _hyb_out_ffn_kernel(oa_ref, ob_ref, wo_ref, *rest):
    m = _dot(oa_ref[...], wo_ref[:DA, :]) + _dot(ob_ref[...], wo_ref[DA:, :])
    _residual_ffn(m, *rest)


def _glu_out_ffn_kernel(y_ref, wglu_ref, *rest):
    d = wglu_ref.shape[1] // 2
    yv = _dot(y_ref[...], wglu_ref[...])
    m = yv[:, :d] * _sigmoid(yv[:, d:])
    _residual_ffn(m, *rest)


def _mixer_out_ffn(kernel_fn, acts, w_mix, h, npost, nfpre, wg, wu, wd, nfpost, tm=256):
    t, d = h.shape
    dff = wg.shape[1]
    row = lambda width: pl.BlockSpec((tm, width), lambda i: (i, 0))
    return pl.pallas_call(
        kernel_fn,
        grid=(t // tm,),
        in_specs=[row(a.shape[1]) for a in acts] + [
            _const_spec(w_mix.shape), row(d), _const_spec((1, d)), _const_spec((1, d)),
            _const_spec((d, dff)), _const_spec((d, dff)), _const_spec((dff, d)), _const_spec((1, d))],
        out_specs=row(d),
        out_shape=jax.ShapeDtypeStruct((t, d), F32),
        compiler_params=_params("parallel"),
        name="mixer_out_ffn",
    )(*acts, w_mix, h, npost.reshape(1, d), nfpre.reshape(1, d), wg, wu, wd, nfpost.reshape(1, d))


def _s5_kernel(hs_ref, bmat_ref, cmat_ref, a_ref, d_ref, y_ref, state_ref, bu_ref):
    tl, bsz, width = hs_ref.shape
    rows = tl * bsz
    n_tiles = width // LANE

    @pl.when(pl.program_id(0) == 0)
    def _():
        state_ref[...] = jnp.zeros_like(state_ref)

    def channel_tile(k):
        return hs_ref[:, :, k * LANE:(k + 1) * LANE].reshape(rows, LANE)

    def input_map(k):
        bu_ref[k % 2] = _dot(channel_tile(k).astype(BF16), bmat_ref[k])

    input_map(0)
    for k in range(n_tiles):
        if k + 1 < n_tiles:
            input_map(k + 1)
        buf = bu_ref.at[k % 2]
        a_re = jnp.broadcast_to(a_ref[k, 0:1, :], (bsz, S5_HALF))
        a_im = jnp.broadcast_to(a_ref[k, 1:2, :], (bsz, S5_HALF))
        st = state_ref[k]
        x_re, x_im = st[:, :S5_HALF], st[:, S5_HALF:]
        for t in range(tl):
            rs = slice(t * bsz, (t + 1) * bsz)
            bu = buf[rs, :]
            x_re, x_im = (a_re * x_re - a_im * x_im + bu[:, :S5_HALF],
                          a_re * x_im + a_im * x_re + bu[:, S5_HALF:])
            buf[rs, :] = jnp.concatenate([x_re, x_im], axis=1)
        state_ref[k] = jnp.concatenate([x_re, x_im], axis=1)

        lanes = slice(k * LANE, (k + 1) * LANE)
        y = _dot(buf[...].astype(BF16), cmat_ref[k]) + d_ref[:, lanes] * channel_tile(k)
        y_ref[:, :, lanes] = jax.nn.gelu(y).reshape(tl, bsz, LANE).astype(y_ref.dtype)


def _s5_scan(hs_t, bmat, cmat, a_bar, d_skip):
    lp, bsz, width = hs_t.shape
    tl = S5_TILE
    n_tiles = width // LANE
    return pl.pallas_call(
        _s5_kernel,
        grid=(lp // tl,),
        in_specs=[pl.BlockSpec((tl, bsz, width), lambda i: (i, 0, 0)),
                  _const_spec(bmat.shape), _const_spec(cmat.shape),
                  _const_spec(a_bar.shape), _const_spec((1, width))],
        out_specs=pl.BlockSpec((tl, bsz, width), lambda i: (i, 0, 0)),
        out_shape=jax.ShapeDtypeStruct((lp, bsz, width), BF16),
        scratch_shapes=[pltpu.VMEM((n_tiles, bsz, 2 * S5_HALF), F32),
                        pltpu.VMEM((2, tl * bsz, 2 * S5_HALF), F32)],
        compiler_params=_params("arbitrary"),
        name="s5_scan",
    )(hs_t, bmat, cmat, a_bar, d_skip.reshape(1, width))


def _s5_discretise(lam_re, lam_im, log_step, b_re, b_im, c_re, c_im):
    n_groups = lam_re.shape[0]
    n_tiles = n_groups // GROUPS_PER_LANE_TILE
    dt = jnp.exp(log_step)[:, None]
    mag = jnp.exp(lam_re * dt)
    a_re = mag * jnp.cos(lam_im * dt)
    a_im = mag * jnp.sin(lam_im * dt)
    den = lam_re * lam_re + lam_im * lam_im
    f_re = ((a_re - 1.0) * lam_re + a_im * lam_im) / den
    f_im = (a_im * lam_re - (a_re - 1.0) * lam_im) / den
    bb_re = f_re[..., None] * b_re - f_im[..., None] * b_im
    bb_im = f_re[..., None] * b_im + f_im[..., None] * b_re
    eye = jnp.eye(GROUPS_PER_LANE_TILE, dtype=F32)

    def in_map(bb):
        t = bb.reshape(n_tiles, GROUPS_PER_LANE_TILE, STATE, GROUP)
        t = jnp.einsum("kgpc,gh->kgchp", t, eye)
        return t.reshape(n_tiles, LANE, S5_HALF)

    def out_map(cc):
        t = cc.reshape(n_tiles, GROUPS_PER_LANE_TILE, GROUP, STATE)
        t = jnp.einsum("kgcp,gh->kgphc", t, eye)
        return t.reshape(n_tiles, S5_HALF, LANE)

    bmat = jnp.concatenate([in_map(bb_re), in_map(bb_im)], axis=2).astype(BF16)
    cmat = jnp.concatenate([out_map(c_re), -out_map(c_im)], axis=1).astype(BF16)
    a_bar = jnp.stack([a_re.reshape(n_tiles, S5_HALF), a_im.reshape(n_tiles, S5_HALF)], axis=1)
    return bmat, cmat, a_bar


def kernel(x, meta_tokens, norm_mix_pre, norm_mix_post, norm_ffn_pre, norm_ffn_post, ffn_w_gate,
           ffn_w_up, ffn_w_down, hyb_w_in, hyb_conv_w, dn_a_log, dn_dt_bias, dn_out_norm, hyb_w_out,
           ssm_w_in, ssm_lambda_re, ssm_lambda_im, ssm_log_step, ssm_b_re, ssm_b_im, ssm_c_re,
           ssm_c_im, ssm_d, ssm_w_glu):
    bsz, seq, d = x.shape
    l_real = N_META + seq
    lp = -(-l_real // SEQ_ALIGN) * SEQ_ALIGN
    t = bsz * lp
    meta = jnp.broadcast_to(meta_tokens[None].astype(x.dtype), (bsz, N_META, d))
    h = jnp.concatenate([meta, x, jnp.zeros((bsz, lp - l_real, d), x.dtype)], axis=1).reshape(t, d)

    w_in = hyb_w_in[0]
    n_gate = 2 * HA
    w_cat = jnp.concatenate([
        w_in[:, :4 * DA],
        jnp.pad(w_in[:, 4 * DA:4 * DA + n_gate], ((0, 0), (0, LANE - n_gate))),
        w_in[:, 4 * DA + n_gate:]], axis=1).astype(BF16)
    gate_par = jnp.zeros((2, LANE), F32)
    gate_par = gate_par.at[0, HA:2 * HA].set(dn_a_log[0].astype(F32))
    gate_par = gate_par.at[1, HA:2 * HA].set(dn_dt_bias[0].astype(F32))
    qkv_a, zs_a, gates, qkv_b = _hyb_in(h.reshape(bsz, lp, d), norm_mix_pre[0], w_cat,
                                        hyb_conv_w[0].astype(F32), gate_par)
    o_a = _deltanet(qkv_a, zs_a, gates, dn_out_norm[0].reshape(1, HEAD_DIM).astype(F32))
    o_b = _stickbreak(qkv_b)
    h = _mixer_out_ffn(_hyb_out_ffn_kernel, [o_a.reshape(t, DA), o_b.reshape(t, DB)],
                       hyb_w_out[0].astype(BF16), h, norm_mix_post[0], norm_ffn_pre[0],
                       ffn_w_gate[0].astype(BF16), ffn_w_up[0].astype(BF16),
                       ffn_w_down[0].astype(BF16), norm_ffn_post[0])

    width = ssm_w_in.shape[2]
    (hs,) = _norm_proj(h, norm_mix_pre[1], ssm_w_in[0].astype(BF16), ((0, width),), (F32,))
    bmat, cmat, a_bar = _s5_discretise(ssm_lambda_re[0].astype(F32), ssm_lambda_im[0].astype(F32),
                                       ssm_log_step[0].astype(F32), ssm_b_re[0].astype(F32),
                                       ssm_b_im[0].astype(F32), ssm_c_re[0].astype(F32),
                                       ssm_c_im[0].astype(F32))
    hs_t = jnp.transpose(hs.reshape(bsz, lp, width), (1, 0, 2))
    y_t = _s5_scan(hs_t, bmat, cmat, a_bar, ssm_d[0].astype(F32))
    y = jnp.transpose(y_t, (1, 0, 2)).reshape(t, width)
    h = _mixer_out_ffn(_glu_out_ffn_kernel, [y], ssm_w_glu[0].astype(BF16), h, norm_mix_post[1],
                       norm_ffn_pre[1], ffn_w_gate[1].astype(BF16), ffn_w_up[1].astype(BF16),
                       ffn_w_down[1].astype(BF16), norm_ffn_post[1])
    return h.reshape(bsz, lp, d)[:, N_META:l_real]
```

```python
import functools

import jax
import jax.numpy as jnp
from jax import lax
from jax.experimental import pallas as pl
from jax.experimental.pallas import tpu as pltpu

N_META = 16
HEAD_DIM = 128
HA = 4
HB = 4
DA = HA * HEAD_DIM
DB = HB * HEAD_DIM
CONV_WIDTH = 4
GROUP = 16
STATE = 64
EPS = 1e-6

LANE = 128
SEQ_ALIGN = 128
DN_CHUNK = 128
DN_SEQS_PER_STEP = 2
INV_BASE = 16
ATT_BLOCK = 128
ATT_FIRST_BLOCKS = 3
FFN_SUBTILE = 256
S5_TILE = 64
GROUPS_PER_LANE_TILE = LANE // GROUP
S5_HALF = GROUPS_PER_LANE_TILE * STATE
VMEM_LIMIT = 56 * 1024 * 1024
EXP_UNDERFLOW = -105.0

F32 = jnp.float32
BF16 = jnp.bfloat16
HIGHEST = lax.Precision.HIGHEST


def _dot(a, b):
    return jnp.dot(a, b, preferred_element_type=F32)


def _dot_nt(a, b):
    return lax.dot_general(a, b, (((1,), (1,)), ((), ())), preferred_element_type=F32)


def _dot_tn(a, b):
    return lax.dot_general(a, b, (((0,), (0,)), ((), ())), preferred_element_type=F32)


def _dot_f32(a, b):
    return jnp.dot(a, b, preferred_element_type=F32, precision=HIGHEST)


def _rms(x, w):
    return x * lax.rsqrt(jnp.mean(x * x, axis=-1, keepdims=True) + EPS) * w


def _sigmoid(x):
    return 1.0 / (1.0 + jnp.exp(-x))


def _softplus(x):
    return jnp.maximum(x, 0.0) + jnp.log(1.0 + jnp.exp(-jnp.abs(x)))


def _const_spec(shape):
    zeros = (0,) * len(shape)
    return pl.BlockSpec(shape, lambda *_: zeros, pipeline_mode=pl.Buffered(1))


def _params(*sem):
    return pltpu.CompilerParams(dimension_semantics=sem, vmem_limit_bytes=VMEM_LIMIT)


def _norm_proj_kernel(x_ref, nw_ref, w_ref, *out_refs, col_splits):
    y = _rms(x_ref[...], nw_ref[...]).astype(BF16)
    for out_ref, (start, width) in zip(out_refs, col_splits):
        out_ref[...] = _dot(y, w_ref[:, start:start + width]).astype(out_ref.dtype)


def _norm_proj(x, norm_w, w, col_splits, out_dtypes, tm=512):
    t, d = x.shape
    n = w.shape[1]
    return pl.pallas_call(
        functools.partial(_norm_proj_kernel, col_splits=col_splits),
        grid=(t // tm,),
        in_specs=[pl.BlockSpec((tm, d), lambda i: (i, 0)),
                  _const_spec((1, d)),
                  _const_spec((d, n))],
        out_specs=[pl.BlockSpec((tm, width), lambda i: (i, 0)) for _, width in col_splits],
        out_shape=[jax.ShapeDtypeStruct((t, width), dt)
                   for (_, width), dt in zip(col_splits, out_dtypes)],
        compiler_params=_params("parallel"),
        name="norm_proj",
    )(x, norm_w.reshape(1, d), w)


HYB_COLS = (("qkv_a", 0, 3 * DA), ("z_a", 3 * DA, DA), ("gates", 4 * DA, LANE),
            ("qkv_b", 4 * DA + LANE, 3 * DB))
HALO = 8


def _hyb_in_kernel(x_ref, nw_ref, w_ref, convw_ref, gpar_ref, qkv_ref, zs_ref, gate_ref, qkvb_ref,
                   tail_ref):
    cols = {name: slice(start, start + width) for name, start, width in HYB_COLS}

    @pl.when(pl.program_id(1) == 0)
    def _():
        tail_ref[...] = jnp.zeros_like(tail_ref)

    y = _rms(x_ref[0], nw_ref[...]).astype(BF16)
    first_rows = lax.broadcasted_iota(jnp.int32, (HALO, DA), 0)

    def conv_silu(pre, cs):
        tail = tail_ref[:, cs]
        acc = pre * convw_ref[CONV_WIDTH - 1:CONV_WIDTH, cs]
        for s in range(1, CONV_WIDTH):
            rolled = pltpu.roll(pre, s, 0)
            head = jnp.where(first_rows < s, pltpu.roll(tail, s, 0), rolled[:HALO])
            shifted = jnp.concatenate([head, rolled[HALO:]], axis=0)
            acc = acc + shifted * convw_ref[CONV_WIDTH - 1 - s:CONV_WIDTH - s, cs]
        tail_ref[:, cs] = pre[pre.shape[0] - HALO:, :]
        return acc * _sigmoid(acc)

    def unit_heads(xc, base, gain):
        for h in range(HA):
            t = xc[:, h * HEAD_DIM:(h + 1) * HEAD_DIM]
            t = t * (lax.rsqrt(jnp.sum(t * t, axis=-1, keepdims=True) + EPS) * gain)
            qkv_ref[0, :, base + h * HEAD_DIM:base + (h + 1) * HEAD_DIM] = t.astype(qkv_ref.dtype)

    def post_q(pre):
        unit_heads(conv_silu(pre, slice(0, DA)), 0, HEAD_DIM ** -0.5)

    def post_k(pre):
        unit_heads(conv_silu(pre, slice(DA, 2 * DA)), DA, 1.0)

    def post_v(pre):
        qkv_ref[0, :, 2 * DA:] = conv_silu(pre, slice(2 * DA, 3 * DA)).astype(qkv_ref.dtype)

    def post_z(zlin):
        zs_ref[0] = (zlin * _sigmoid(zlin)).astype(zs_ref.dtype)

    def post_gates(ba):
        g = -jnp.exp(gpar_ref[0:1, :]) * _softplus(ba + gpar_ref[1:2, :])
        gate_ref[0] = jnp.where(lax.broadcasted_iota(jnp.int32, ba.shape, 1) < HA, _sigmoid(ba), g)

    def post_b(part):
        def store(pre):
            qkvb_ref[0, :, part * DB:(part + 1) * DB] = pre.astype(qkvb_ref.dtype)
        return store

    qa, qb = cols["qkv_a"].start, cols["qkv_b"].start
    stages = [(slice(qa, qa + DA), post_q), (slice(qa + DA, qa + 2 * DA), post_k),
              (slice(qa + 2 * DA, qa + 3 * DA), post_v), (cols["z_a"], post_z),
              (cols["gates"], post_gates)]
    stages += [(slice(qb + part * DB, qb + (part + 1) * DB), post_b(part)) for part in range(3)]
    pending = _dot(y, w_ref[:, stages[0][0]])
    for idx, (_, post) in enumerate(stages):
        upcoming = _dot(y, w_ref[:, stages[idx + 1][0]]) if idx + 1 < len(stages) else None
        post(pending)
        pending = upcoming


def _hyb_in(h, norm_w, w, conv_w, gate_par):
    b, lp, d = h.shape
    tm = next(rows for rows in (384, 256, 128) if lp % rows == 0)
    row = lambda width: pl.BlockSpec((1, tm, width), lambda bi, i: (bi, i, 0))
    widths = {name: width for name, _, width in HYB_COLS}
    return pl.pallas_call(
        _hyb_in_kernel,
        grid=(b, lp // tm),
        in_specs=[row(d), _const_spec((1, d)), _const_spec(w.shape),
                  _const_spec(conv_w.shape), _const_spec(gate_par.shape)],
        out_specs=[row(widths["qkv_a"]), row(widths["z_a"]), row(widths["gates"]), row(widths["qkv_b"])],
        out_shape=[jax.ShapeDtypeStruct((b, lp, widths["qkv_a"]), BF16),
                   jax.ShapeDtypeStruct((b, lp, widths["z_a"]), BF16),
                   jax.ShapeDtypeStruct((b, lp, widths["gates"]), F32),
                   jax.ShapeDtypeStruct((b, lp, widths["qkv_b"]), BF16)],
        scratch_shapes=[pltpu.VMEM((HALO, widths["qkv_a"]), F32)],
        compiler_params=_params("parallel", "arbitrary"),
        name="hyb_in",
    )(h, norm_w.reshape(1, d), w, conv_w, gate_par)


def _unit_lower_inverses(lowers, ri, ci):
    c = lowers[0].shape[0]

    def same_block(size):
        shift = size.bit_length() - 1
        return (ri >> shift) == (ci >> shift)

    def mm(a, b):
        return _dot(a.astype(BF16), b.astype(BF16))

    eye = jnp.where(ci == ri, 1.0, 0.0).astype(F32)
    in_base = same_block(INV_BASE)
    xs = [jnp.where(in_base, -low, 0.0) for low in lowers]
    ps = [eye + x for x in xs]
    power = 2
    while power < INV_BASE:
        xs = [mm(x, x) for x in xs]
        ps = [p + mm(p, x) for p, x in zip(ps, xs)]
        power *= 2
    size = INV_BASE
    while size < c:
        joins = jnp.logical_and(same_block(2 * size), jnp.logical_not(same_block(size)))
        ys = [mm(jnp.where(joins, low, 0.0), p) for low, p in zip(lowers, ps)]
        ps = [p - mm(p, y) for p, y in zip(ps, ys)]
        size *= 2
    return ps


def _deltanet_kernel(qkv_ref, zs_ref, gate_ref, onorm_ref, o_ref, state_ref):
    nseq, c = qkv_ref.shape[0], qkv_ref.shape[1]
    units = [(s, h) for s in range(nseq) for h in range(HA)]
    n = range(len(units))

    @pl.when(pl.program_id(1) == 0)
    def _():
        state_ref[...] = jnp.zeros_like(state_ref)

    ri = lax.broadcasted_iota(jnp.int32, (c, c), 0)
    ci = lax.broadcasted_iota(jnp.int32, (c, c), 1)
    incl = ci <= ri
    strict = ci < ri
    tril = jnp.where(incl, 1.0, 0.0).astype(F32)
    sel = jnp.where(lax.broadcasted_iota(jnp.int32, (8, LANE), 1)
                    == lax.broadcasted_iota(jnp.int32, (8, LANE), 0) + HA, 1.0, 0.0).astype(F32)
    gates = [gate_ref[s] for s in range(nseq)]
    g_cum = [_dot_f32(tril, g) for g in gates]
    g_rows = [lax.dot_general(sel, g, (((1,), (1,)), ((), ())), preferred_element_type=F32,
                              precision=HIGHEST) for g in g_cum]

    def head_cols(base, s, h):
        return qkv_ref[s, :, base + h * HEAD_DIM:base + (h + 1) * HEAD_DIM]

    q_bf = [head_cols(0, s, h) for s, h in units]
    k_bf = [head_cols(DA, s, h) for s, h in units]
    q = [t.astype(F32) for t in q_bf]
    k = [t.astype(F32) for t in k_bf]
    v = [head_cols(2 * DA, s, h).astype(F32) for s, h in units]
    beta = [gates[s][:, h:h + 1] for s, h in units]
    g_col = [g_cum[s][:, HA + h:HA + h + 1] for s, h in units]
    g_row = [g_rows[s][h:h + 1, :] for s, h in units]
    g_last = [g_cum[s][c - 1:c, HA + h:HA + h + 1] for s, h in units]
    decay = [jnp.where(incl, jnp.exp(jnp.where(incl, g_col[u] - g_row[u], 0.0)), 0.0) for u in n]
    exp_g = [jnp.exp(g) for g in g_col]
    kb = [k[u] * beta[u] for u in n]
    lower = [jnp.where(strict, _dot_nt(kb[u].astype(BF16), k_bf[u]) * decay[u], 0.0) for u in n]
    qk = [jnp.where(incl, _dot_nt(q_bf[u], k_bf[u]) * decay[u], 0.0).astype(BF16) for u in n]
    t_inv = _unit_lower_inverses(lower, ri, ci)
    rhs = [jnp.concatenate([v[u] * beta[u], kb[u] * exp_g[u]], axis=1).astype(BF16) for u in n]
    sol = [_dot(t_inv[u].astype(BF16), rhs[u]) for u in n]
    q_dec = [(q[u] * exp_g[u]).astype(BF16) for u in n]
    k_dec = [(k[u] * jnp.exp(g_last[u] - g_col[u])).astype(BF16) for u in n]

    state = [state_ref[u] for u in n]
    state_bf = [s.astype(BF16) for s in state]
    v_new = [(sol[u][:, :HEAD_DIM] - _dot(sol[u][:, HEAD_DIM:].astype(BF16), state_bf[u])).astype(BF16)
             for u in n]
    o = [_dot(q_dec[u], state_bf[u]) + _dot(qk[u], v_new[u]) for u in n]
    for u in n:
        state_ref[u] = state[u] * jnp.exp(g_last[u]) + _dot_tn(k_dec[u], v_new[u])

    onorm = onorm_ref[...]
    for u, (s, h) in enumerate(units):
        sl = slice(h * HEAD_DIM, (h + 1) * HEAD_DIM)
        o_ref[s, :, sl] = (_rms(o[u], onorm) * zs_ref[s, :, sl].astype(F32)).astype(o_ref.dtype)


def _deltanet(qkv, zs, gates, out_norm):
    b, lp, _ = qkv.shape
    c = DN_CHUNK
    nseq = DN_SEQS_PER_STEP
    return pl.pallas_call(
        _deltanet_kernel,
        grid=(b // nseq, lp // c),
        in_specs=[pl.BlockSpec((nseq, c, 3 * DA), lambda bi, i: (bi, i, 0)),
                  pl.BlockSpec((nseq, c, DA), lambda bi, i: (bi, i, 0)),
                  pl.BlockSpec((nseq, c, LANE), lambda bi, i: (bi, i, 0)),
                  _const_spec((1, HEAD_DIM))],
        out_specs=pl.BlockSpec((nseq, c, DA), lambda bi, i: (bi, i, 0)),
        out_shape=jax.ShapeDtypeStruct((b, lp, DA), BF16),
        scratch_shapes=[pltpu.VMEM((nseq * HA, HEAD_DIM, HEAD_DIM), F32)],
        compiler_params=_params("parallel", "arbitrary"),
        name="deltanet",
    )(qkv, zs, gates, out_norm)


def _stickbreak_visit(q_ref, k_ref, v_ref, heads, suffix, first, count, laters, diag_mask):
    tk = suffix.shape[0]
    n_heads = range(len(heads))
    chains = [(b, h) for b in range(count) for h in n_heads]
    index = [first - b for b in range(count)]
    starts = [pl.multiple_of(jnp.maximum(j, 0) * tk, tk) for j in index]

    def masked(c, t):
        b, _ = chains[c]
        if diag_mask is None:
            return t
        return jnp.where(diag_mask if b == 0 else index[b] >= 0, t, 0.0)

    n = range(len(chains))
    qs = [q_ref[0, :, sl] for sl in heads]
    zq = [_dot_nt(qs[h], k_ref[0, pl.ds(starts[b], tk), heads[h]]) * (HEAD_DIM ** -0.5)
          for b, h in chains]
    log_beta = [jnp.minimum(z, 0.0) - jnp.log(1.0 + jnp.exp(-jnp.abs(z))) for z in zq]
    log_1m = [masked(c, log_beta[c] - zq[c]) for c in n]
    hi = [t.astype(BF16) for t in log_1m]
    lo = [(log_1m[c] - hi[c].astype(F32)).astype(BF16) for c in n]
    suf = [_dot(hi[c], suffix) + _dot(lo[c], suffix) for c in n]
    entering, running = {}, list(laters)
    for c, (b, h) in enumerate(chains):
        entering[c] = running[h]
        running[h] = running[h] + suf[c][:, 0:1]
    wgt = [masked(c, jnp.exp(log_beta[c] + (suf[c] - log_1m[c]) + entering[c])).astype(BF16) for c in n]
    pv = [_dot(wgt[c], v_ref[0, pl.ds(starts[b], tk), heads[h]]) for c, (b, h) in enumerate(chains)]
    contrib = tuple(functools.reduce(lambda x, y: x + y, [pv[c] for c in n if chains[c][1] == h])
                    for h in n_heads)
    largest = jnp.max(functools.reduce(jnp.maximum, running))
    return contrib, tuple(running), largest


def _stickbreak_kernel(q_ref, k_ref, v_ref, o_ref):
    tq = q_ref.shape[1]
    tk = tq
    i = pl.program_id(1)
    heads = [slice(h * HEAD_DIM, (h + 1) * HEAD_DIM) for h in range(HB)]
    suffix = jnp.where(lax.broadcasted_iota(jnp.int32, (tk, tk), 0)
                       >= lax.broadcasted_iota(jnp.int32, (tk, tk), 1), 1.0, 0.0).astype(BF16)
    strictly_causal = (lax.broadcasted_iota(jnp.int32, (tq, tk), 1)
                       < lax.broadcasted_iota(jnp.int32, (tq, tk), 0))
    accs, laters, largest = _stickbreak_visit(
        q_ref, k_ref, v_ref, heads, suffix, i, ATT_FIRST_BLOCKS,
        (jnp.zeros((tq, 1), F32),) * HB, strictly_causal)

    def more(carry):
        j, largest, _, _ = carry
        return jnp.logical_and(j >= 0, largest > EXP_UNDERFLOW)

    def body(carry):
        j, _, accs, laters = carry
        contrib, laters, largest = _stickbreak_visit(q_ref, k_ref, v_ref, heads, suffix, j, 1,
                                                     laters, None)
        return j - 1, largest, tuple(a + c for a, c in zip(accs, contrib)), laters

    _, _, accs, _ = lax.while_loop(more, body, (i - ATT_FIRST_BLOCKS, largest, accs, laters))
    for sl, acc in zip(heads, accs):
        o_ref[0, :, sl] = acc.astype(o_ref.dtype)


def _stickbreak(qkv):
    b, lp, _ = qkv.shape
    tq = ATT_BLOCK
    return pl.pallas_call(
        _stickbreak_kernel,
        grid=(b, lp // tq),
        in_specs=[pl.BlockSpec((1, tq, DB), lambda bi, i: (bi, i, 0)),
                  pl.BlockSpec((1, lp, DB), lambda bi, i: (bi, 0, 1)),
                  pl.BlockSpec((1, lp, DB), lambda bi, i: (bi, 0, 2))],
        out_specs=pl.BlockSpec((1, tq, DB), lambda bi, i: (bi, i, 0)),
        out_shape=jax.ShapeDtypeStruct((b, lp, DB), BF16),
        compiler_params=_params("parallel", "arbitrary"),
        name="stickbreak",
    )(qkv, qkv, qkv)


def _row_subtiles(rows):
    return [slice(r, r + FFN_SUBTILE) for r in range(0, rows, FFN_SUBTILE)]


def _residual_ffn(ms, subs, h_ref, npost_ref, nfpre_ref, wg_ref, wu_ref, wd_ref, nfpost_ref, out_ref):
    h1 = [h_ref[r, :] + _rms(m, npost_ref[...]) for m, r in zip(ms, subs)]
    hn = [_rms(t, nfpre_ref[...]).astype(BF16) for t in h1]
    g = [_dot(t, wg_ref[...]) for t in hn]
    u = [_dot(t, wu_ref[...]) for t in hn]
    a = [(gi * _sigmoid(gi) * ui).astype(BF16) for gi, ui in zip(g, u)]
    f = [_dot(t, wd_ref[...]) for t in a]
    for r, h1i, fi in zip(subs, h1, f):
        out_ref[r, :] = h1i + _rms(fi, nfpost_ref[...])


def _hyb_out_ffn_kernel(oa_ref, ob_ref, wo_ref, *rest):
    subs = _row_subtiles(oa_ref.shape[0])
    ms = [_dot(oa_ref[r, :], wo_ref[:DA, :]) + _dot(ob_ref[r, :], wo_ref[DA:, :]) for r in subs]
    _residual_ffn(ms, subs, *rest)


def _glu_out_ffn_kernel(y_ref, wglu_ref, *rest):
    d = wglu_ref.shape[1] // 2
    subs = _row_subtiles(y_ref.shape[0])
    yv = [_dot(y_ref[r, :], wglu_ref[...]) for r in subs]
    ms = [t[:, :d] * _sigmoid(t[:, d:]) for t in yv]
    _residual_ffn(ms, subs, *rest)


def _mixer_out_ffn(kernel_fn, acts, w_mix, h, npost, nfpre, wg, wu, wd, nfpost, tm=2 * FFN_SUBTILE):
    t, d = h.shape
    dff = wg.shape[1]
    row = lambda width: pl.BlockSpec((tm, width), lambda i: (i, 0))
    return pl.pallas_call(
        kernel_fn,
        grid=(t // tm,),
        in_specs=[row(a.shape[1]) for a in acts] + [
            _const_spec(w_mix.shape), row(d), _const_spec((1, d)), _const_spec((1, d)),
            _const_spec((d, dff)), _const_spec((d, dff)), _const_spec((dff, d)), _const_spec((1, d))],
        out_specs=row(d),
        out_shape=jax.ShapeDtypeStruct((t, d), F32),
        compiler_params=_params("parallel"),
        name="mixer_out_ffn",
    )(*acts, w_mix, h, npost.reshape(1, d), nfpre.reshape(1, d), wg, wu, wd, nfpost.reshape(1, d))


def _s5_kernel(hs_ref, bmat_ref, cmat_ref, a_ref, d_ref, y_ref, state_ref, bu_ref):
    tl, bsz, width = hs_ref.shape
    rows = tl * bsz
    n_tiles = width // LANE

    @pl.when(pl.program_id(0) == 0)
    def _():
        state_ref[...] = jnp.zeros_like(state_ref)

    def channel_tile(k):
        return hs_ref[:, :, k * LANE:(k + 1) * LANE].reshape(rows, LANE)

    def input_map(k):
        bu_ref[k % 2] = _dot(channel_tile(k).astype(BF16), bmat_ref[k])

    input_map(0)
    for k in range(n_tiles):
        if k + 1 < n_tiles:
            input_map(k + 1)
        buf = bu_ref.at[k % 2]
        a_re = jnp.broadcast_to(a_ref[k, 0:1, :], (bsz, S5_HALF))
        a_im = jnp.broadcast_to(a_ref[k, 1:2, :], (bsz, S5_HALF))
        st = state_ref[k]
        x_re, x_im = st[:, :S5_HALF], st[:, S5_HALF:]
        for t in range(tl):
            rs = slice(t * bsz, (t + 1) * bsz)
            bu = buf[rs, :]
            x_re, x_im = (a_re * x_re - a_im * x_im + bu[:, :S5_HALF],
                          a_re * x_im + a_im * x_re + bu[:, S5_HALF:])
            buf[rs, :] = jnp.concatenate([x_re, x_im], axis=1)
        state_ref[k] = jnp.concatenate([x_re, x_im], axis=1)

        lanes = slice(k * LANE, (k + 1) * LANE)
        y = _dot(buf[...].astype(BF16), cmat_ref[k]) + d_ref[:, lanes] * channel_tile(k)
        y_ref[:, :, lanes] = jax.nn.gelu(y).reshape(tl, bsz, LANE).astype(y_ref.dtype)


def _s5_scan(hs_t, bmat, cmat, a_bar, d_skip):
    lp, bsz, width = hs_t.shape
    tl = S5_TILE
    n_tiles = width // LANE
    return pl.pallas_call(
        _s5_kernel,
        grid=(lp // tl,),
        in_specs=[pl.BlockSpec((tl, bsz, width), lambda i: (i, 0, 0)),
                  _const_spec(bmat.shape), _const_spec(cmat.shape),
                  _const_spec(a_bar.shape), _const_spec((1, width))],
        out_specs=pl.BlockSpec((tl, bsz, width), lambda i: (i, 0, 0)),
        out_shape=jax.ShapeDtypeStruct((lp, bsz, width), BF16),
        scratch_shapes=[pltpu.VMEM((n_tiles, bsz, 2 * S5_HALF), F32),
                        pltpu.VMEM((2, tl * bsz, 2 * S5_HALF), F32)],
        compiler_params=_params("arbitrary"),
        name="s5_scan",
    )(hs_t, bmat, cmat, a_bar, d_skip.reshape(1, width))


def _s5_discretise_kernel(lre_ref, lim_ref, step_ref, bre_ref, bim_ref,
                          are_ref, aim_ref, bbre_ref, bbim_ref):
    lam_re, lam_im = lre_ref[...], lim_ref[...]
    dt = jnp.exp(step_ref[...])
    mag = jnp.exp(lam_re * dt)
    a_re = mag * jnp.cos(lam_im * dt)
    a_im = mag * jnp.sin(lam_im * dt)
    den = lam_re * lam_re + lam_im * lam_im
    f_re = ((a_re - 1.0) * lam_re + a_im * lam_im) / den
    f_im = (a_im * lam_re - (a_re - 1.0) * lam_im) / den
    are_ref[...] = a_re
    aim_ref[...] = a_im
    bbre_ref[...] = f_re * bre_ref[...] - f_im * bim_ref[...]
    bbim_ref[...] = f_re * bim_ref[...] + f_im * bre_ref[...]


def _s5_discretise(lam_re, lam_im, log_step, b_re, b_im, c_re, c_im):
    n_groups = lam_re.shape[0]
    n_tiles = n_groups // GROUPS_PER_LANE_TILE
    flat = (n_groups, STATE * GROUP)
    spread = lambda t: jnp.repeat(t, GROUP, axis=1)
    a_re, a_im, bb_re, bb_im = pl.pallas_call(
        _s5_discretise_kernel,
        out_shape=[jax.ShapeDtypeStruct(flat, F32)] * 4,
        name="s5_discretise",
    )(spread(lam_re), spread(lam_im), log_step.reshape(n_groups, 1),
      b_re.reshape(flat), b_im.reshape(flat))
    a_re, a_im = a_re[:, ::GROUP], a_im[:, ::GROUP]
    bb_re = bb_re.reshape(n_groups, STATE, GROUP)
    bb_im = bb_im.reshape(n_groups, STATE, GROUP)
    eye = jnp.eye(GROUPS_PER_LANE_TILE, dtype=F32)

    def in_map(bb):
        t = bb.reshape(n_tiles, GROUPS_PER_LANE_TILE, STATE, GROUP)
        t = jnp.einsum("kgpc,gh->kgchp", t, eye)
        return t.reshape(n_tiles, LANE, S5_HALF)

    def out_map(cc):
        t = cc.reshape(n_tiles, GROUPS_PER_LANE_TILE, GROUP, STATE)
        t = jnp.einsum("kgcp,gh->kgphc", t, eye)
        return t.reshape(n_tiles, S5_HALF, LANE)

    bmat = jnp.concatenate([in_map(bb_re), in_map(bb_im)], axis=2).astype(BF16)
    cmat = jnp.concatenate([out_map(c_re), -out_map(c_im)], axis=1).astype(BF16)
    a_bar = jnp.stack([a_re.reshape(n_tiles, S5_HALF), a_im.reshape(n_tiles, S5_HALF)], axis=1)
    return bmat, cmat, a_bar


def kernel(x, meta_tokens, norm_mix_pre, norm_mix_post, norm_ffn_pre, norm_ffn_post, ffn_w_gate,
           ffn_w_up, ffn_w_down, hyb_w_in, hyb_conv_w, dn_a_log, dn_dt_bias, dn_out_norm, hyb_w_out,
           ssm_w_in, ssm_lambda_re, ssm_lambda_im, ssm_log_step, ssm_b_re, ssm_b_im, ssm_c_re,
           ssm_c_im, ssm_d, ssm_w_glu):
    bsz, seq, d = x.shape
    l_real = N_META + seq
    lp = -(-l_real // SEQ_ALIGN) * SEQ_ALIGN
    t = bsz * lp
    meta = jnp.broadcast_to(meta_tokens[None].astype(x.dtype), (bsz, N_META, d))
    h = jnp.concatenate([meta, x, jnp.zeros((bsz, lp - l_real, d), x.dtype)], axis=1).reshape(t, d)

    w_in = hyb_w_in[0]
    n_gate = 2 * HA
    w_cat = jnp.concatenate([
        w_in[:, :4 * DA],
        jnp.pad(w_in[:, 4 * DA:4 * DA + n_gate], ((0, 0), (0, LANE - n_gate))),
        w_in[:, 4 * DA + n_gate:]], axis=1).astype(BF16)
    gate_par = jnp.zeros((2, LANE), F32)
    gate_par = gate_par.at[0, HA:2 * HA].set(dn_a_log[0].astype(F32))
    gate_par = gate_par.at[1, HA:2 * HA].set(dn_dt_bias[0].astype(F32))
    qkv_a, zs_a, gates, qkv_b = _hyb_in(h.reshape(bsz, lp, d), norm_mix_pre[0], w_cat,
                                        hyb_conv_w[0].astype(F32), gate_par)
    o_a = _deltanet(qkv_a, zs_a, gates, dn_out_norm[0].reshape(1, HEAD_DIM).astype(F32))
    o_b = _stickbreak(qkv_b)
    h = _mixer_out_ffn(_hyb_out_ffn_kernel, [o_a.reshape(t, DA), o_b.reshape(t, DB)],
                       hyb_w_out[0].astype(BF16), h, norm_mix_post[0], norm_ffn_pre[0],
                       ffn_w_gate[0].astype(BF16), ffn_w_up[0].astype(BF16),
                       ffn_w_down[0].astype(BF16), norm_ffn_post[0])

    width = ssm_w_in.shape[2]
    (hs,) = _norm_proj(h, norm_mix_pre[1], ssm_w_in[0].astype(BF16), ((0, width),), (F32,))
    bmat, cmat, a_bar = _s5_discretise(ssm_lambda_re[0].astype(F32), ssm_lambda_im[0].astype(F32),
                                       ssm_log_step[0].astype(F32), ssm_b_re[0].astype(F32),
                                       ssm_b_im[0].astype(F32), ssm_c_re[0].astype(F32),
                                       ssm_c_im[0].astype(F32))
    hs_t = jnp.transpose(hs.reshape(bsz, lp, width), (1, 0, 2))
    y_t = _s5_scan(hs_t, bmat, cmat, a_bar, ssm_d[0].astype(F32))
    y = jnp.transpose(y_t, (1, 0, 2)).reshape(t, width)
    h = _mixer_out_ffn(_glu_out_ffn_kernel, [y], ssm_w_glu[0].astype(BF16), h, norm_mix_post[1],
                       norm_ffn_pre[1], ffn_w_gate[1].astype(BF16), ffn_w_up[1].astype(BF16),
                       ffn_w_down[1].astype(BF16), norm_ffn_post[1])
    return h.reshape(bsz, lp, d)[:, N_META:l_real]
```

```python
import functools

import jax
import jax.numpy as jnp
from jax import lax
from jax.experimental import pallas as pl
from jax.experimental.pallas import tpu as pltpu

N_META = 16
HEAD_DIM = 128
HA = 4
HB = 4
DA = HA * HEAD_DIM
DB = HB * HEAD_DIM
CONV_WIDTH = 4
GROUP = 16
STATE = 64
EPS = 1e-6

LANE = 128
SEQ_ALIGN = 128
DN_CHUNK = 128
DN_SEQS_PER_STEP = 2
INV_BASE = 16
ATT_BLOCK = 128
ATT_FIRST_BLOCKS = 3
FFN_SUBTILE = 256
S5_TILE = 64
GROUPS_PER_LANE_TILE = LANE // GROUP
S5_HALF = GROUPS_PER_LANE_TILE * STATE
VMEM_LIMIT = 56 * 1024 * 1024
EXP_UNDERFLOW = -105.0

F32 = jnp.float32
BF16 = jnp.bfloat16
HIGHEST = lax.Precision.HIGHEST


def _dot(a, b):
    return jnp.dot(a, b, preferred_element_type=F32)


def _dot_nt(a, b):
    return lax.dot_general(a, b, (((1,), (1,)), ((), ())), preferred_element_type=F32)


def _dot_tn(a, b):
    return lax.dot_general(a, b, (((0,), (0,)), ((), ())), preferred_element_type=F32)


def _dot_f32(a, b):
    return jnp.dot(a, b, preferred_element_type=F32, precision=HIGHEST)


def _rms(x, w):
    return x * lax.rsqrt(jnp.mean(x * x, axis=-1, keepdims=True) + EPS) * w


def _sigmoid(x):
    return 1.0 / (1.0 + jnp.exp(-x))


def _softplus(x):
    return jnp.maximum(x, 0.0) + jnp.log(1.0 + jnp.exp(-jnp.abs(x)))


def _const_spec(shape):
    zeros = (0,) * len(shape)
    return pl.BlockSpec(shape, lambda *_: zeros, pipeline_mode=pl.Buffered(1))


def _params(*sem):
    return pltpu.CompilerParams(dimension_semantics=sem, vmem_limit_bytes=VMEM_LIMIT)


def _norm_proj_kernel(x_ref, nw_ref, w_ref, *out_refs, col_splits):
    y = _rms(x_ref[...], nw_ref[...]).astype(BF16)
    for out_ref, (start, width) in zip(out_refs, col_splits):
        out_ref[...] = _dot(y, w_ref[:, start:start + width]).astype(out_ref.dtype)


def _norm_proj(x, norm_w, w, col_splits, out_dtypes, tm=512):
    t, d = x.shape
    n = w.shape[1]
    return pl.pallas_call(
        functools.partial(_norm_proj_kernel, col_splits=col_splits),
        grid=(t // tm,),
        in_specs=[pl.BlockSpec((tm, d), lambda i: (i, 0)),
                  _const_spec((1, d)),
                  _const_spec((d, n))],
        out_specs=[pl.BlockSpec((tm, width), lambda i: (i, 0)) for _, width in col_splits],
        out_shape=[jax.ShapeDtypeStruct((t, width), dt)
                   for (_, width), dt in zip(col_splits, out_dtypes)],
        compiler_params=_params("parallel"),
        name="norm_proj",
    )(x, norm_w.reshape(1, d), w)


HYB_COLS = (("qkv_a", 0, 3 * DA), ("z_a", 3 * DA, DA), ("gates", 4 * DA, LANE),
            ("qkv_b", 4 * DA + LANE, 3 * DB))
HALO = 8


def _sequence_tile(x_ref, meta_ref, tile, n_tiles, n_pad):
    xb = x_ref[...]
    tm = xb.shape[0]
    first = jnp.concatenate([meta_ref[...], xb[:tm - N_META]], axis=0)
    if n_pad:
        last = jnp.concatenate([xb[n_pad:], jnp.zeros((n_pad, xb.shape[1]), xb.dtype)], axis=0)
    else:
        last = xb
    return jnp.where(tile == 0, first, jnp.where(tile == n_tiles - 1, last, xb))


def _hyb_in_kernel(x_ref, meta_ref, nw_ref, w_ref, convw_ref, gpar_ref,
                   h_ref, qkv_ref, zs_ref, gate_ref, qkvb_ref, tail_ref, *, n_pad):
    cols = {name: slice(start, start + width) for name, start, width in HYB_COLS}

    @pl.when(pl.program_id(1) == 0)
    def _():
        tail_ref[...] = jnp.zeros_like(tail_ref)

    h0 = _sequence_tile(x_ref, meta_ref, pl.program_id(1), pl.num_programs(1), n_pad)
    h_ref[0] = h0
    y = _rms(h0, nw_ref[...]).astype(BF16)
    first_rows = lax.broadcasted_iota(jnp.int32, (HALO, DA), 0)

    def conv_silu(pre, cs):
        tail = tail_ref[:, cs]
        acc = pre * convw_ref[CONV_WIDTH - 1:CONV_WIDTH, cs]
        for s in range(1, CONV_WIDTH):
            rolled = pltpu.roll(pre, s, 0)
            head = jnp.where(first_rows < s, pltpu.roll(tail, s, 0), rolled[:HALO])
            shifted = jnp.concatenate([head, rolled[HALO:]], axis=0)
            acc = acc + shifted * convw_ref[CONV_WIDTH - 1 - s:CONV_WIDTH - s, cs]
        tail_ref[:, cs] = pre[pre.shape[0] - HALO:, :]
        return acc * _sigmoid(acc)

    def unit_heads(xc, base, gain):
        for h in range(HA):
            t = xc[:, h * HEAD_DIM:(h + 1) * HEAD_DIM]
            t = t * (lax.rsqrt(jnp.sum(t * t, axis=-1, keepdims=True) + EPS) * gain)
            qkv_ref[0, :, base + h * HEAD_DIM:base + (h + 1) * HEAD_DIM] = t.astype(qkv_ref.dtype)

    def post_q(pre):
        unit_heads(conv_silu(pre, slice(0, DA)), 0, HEAD_DIM ** -0.5)

    def post_k(pre):
        unit_heads(conv_silu(pre, slice(DA, 2 * DA)), DA, 1.0)

    def post_v(pre):
        qkv_ref[0, :, 2 * DA:] = conv_silu(pre, slice(2 * DA, 3 * DA)).astype(qkv_ref.dtype)

    def post_z(zlin):
        zs_ref[0] = (zlin * _sigmoid(zlin)).astype(zs_ref.dtype)

    def post_gates(ba):
        g = -jnp.exp(gpar_ref[0:1, :]) * _softplus(ba + gpar_ref[1:2, :])
        gate_ref[0] = jnp.where(lax.broadcasted_iota(jnp.int32, ba.shape, 1) < HA, _sigmoid(ba), g)

    def post_b(part):
        def store(pre):
            qkvb_ref[0, :, part * DB:(part + 1) * DB] = pre.astype(qkvb_ref.dtype)
        return store

    qa, qb = cols["qkv_a"].start, cols["qkv_b"].start
    stages = [(slice(qa, qa + DA), post_q), (slice(qa + DA, qa + 2 * DA), post_k),
              (slice(qa + 2 * DA, qa + 3 * DA), post_v), (cols["z_a"], post_z),
              (cols["gates"], post_gates)]
    stages += [(slice(qb + part * DB, qb + (part + 1) * DB), post_b(part)) for part in range(3)]
    pending = _dot(y, w_ref[:, stages[0][0]])
    for idx, (_, post) in enumerate(stages):
        upcoming = _dot(y, w_ref[:, stages[idx + 1][0]]) if idx + 1 < len(stages) else None
        post(pending)
        pending = upcoming


def _hyb_in(x, meta, lp, norm_w, w, conv_w, gate_par):
    b, seq, d = x.shape
    n_pad = lp - (N_META + seq)
    tm = next(rows for rows in (384, 256, 128) if lp % rows == 0)
    assert lp // tm >= 2 and seq >= tm and n_pad % 8 == 0 and n_pad <= tm
    row = lambda width: pl.BlockSpec((1, tm, width), lambda bi, i: (bi, i, 0))
    n_tiles = lp // tm

    def x_rows(bi, i):
        start = (i * tm - N_META + N_META * (i == 0).astype(jnp.int32)
                 - n_pad * (i == n_tiles - 1).astype(jnp.int32))
        return bi, pl.multiple_of(start, HALO), 0

    x_spec = pl.BlockSpec((None, pl.Element(tm), pl.Element(d)), x_rows)
    widths = {name: width for name, _, width in HYB_COLS}
    return pl.pallas_call(
        functools.partial(_hyb_in_kernel, n_pad=n_pad),
        grid=(b, lp // tm),
        in_specs=[x_spec, _const_spec((N_META, d)), _const_spec((1, d)), _const_spec(w.shape),
                  _const_spec(conv_w.shape), _const_spec(gate_par.shape)],
        out_specs=[row(d), row(widths["qkv_a"]), row(widths["z_a"]), row(widths["gates"]),
                   row(widths["qkv_b"])],
        out_shape=[jax.ShapeDtypeStruct((b, lp, d), F32),
                   jax.ShapeDtypeStruct((b, lp, widths["qkv_a"]), BF16),
                   jax.ShapeDtypeStruct((b, lp, widths["z_a"]), BF16),
                   jax.ShapeDtypeStruct((b, lp, widths["gates"]), F32),
                   jax.ShapeDtypeStruct((b, lp, widths["qkv_b"]), BF16)],
        scratch_shapes=[pltpu.VMEM((HALO, widths["qkv_a"]), F32)],
        compiler_params=_params("parallel", "arbitrary"),
        name="hyb_in",
    )(x, meta, norm_w.reshape(1, d), w, conv_w, gate_par)


def _unit_lower_inverses(lowers, ri, ci):
    c = lowers[0].shape[0]

    def same_block(size):
        shift = size.bit_length() - 1
        return (ri >> shift) == (ci >> shift)

    def mm(a, b):
        return _dot(a.astype(BF16), b.astype(BF16))

    eye = jnp.where(ci == ri, 1.0, 0.0).astype(F32)
    in_base = same_block(INV_BASE)
    xs = [jnp.where(in_base, -low, 0.0) for low in lowers]
    ps = [eye + x for x in xs]
    power = 2
    while power < INV_BASE:
        xs = [mm(x, x) for x in xs]
        ps = [p + mm(p, x) for p, x in zip(ps, xs)]
        power *= 2
    size = INV_BASE
    while size < c:
        joins = jnp.logical_and(same_block(2 * size), jnp.logical_not(same_block(size)))
        ys = [mm(jnp.where(joins, low, 0.0), p) for low, p in zip(lowers, ps)]
        ps = [p - mm(p, y) for p, y in zip(ps, ys)]
        size *= 2
    return ps


def _deltanet_kernel(qkv_ref, zs_ref, gate_ref, onorm_ref, o_ref, state_ref):
    nseq, c = qkv_ref.shape[0], qkv_ref.shape[1]
    units = [(s, h) for s in range(nseq) for h in range(HA)]
    n = range(len(units))

    @pl.when(pl.program_id(1) == 0)
    def _():
        state_ref[...] = jnp.zeros_like(state_ref)

    ri = lax.broadcasted_iota(jnp.int32, (c, c), 0)
    ci = lax.broadcasted_iota(jnp.int32, (c, c), 1)
    incl = ci <= ri
    strict = ci < ri
    tril = jnp.where(incl, 1.0, 0.0).astype(F32)
    sel = jnp.where(lax.broadcasted_iota(jnp.int32, (8, LANE), 1)
                    == lax.broadcasted_iota(jnp.int32, (8, LANE), 0) + HA, 1.0, 0.0).astype(F32)
    gates = [gate_ref[s] for s in range(nseq)]
    g_cum = [_dot_f32(tril, g) for g in gates]
    g_rows = [lax.dot_general(sel, g, (((1,), (1,)), ((), ())), preferred_element_type=F32,
                              precision=HIGHEST) for g in g_cum]

    def head_cols(base, s, h):
        return qkv_ref[s, :, base + h * HEAD_DIM:base + (h + 1) * HEAD_DIM]

    q_bf = [head_cols(0, s, h) for s, h in units]
    k_bf = [head_cols(DA, s, h) for s, h in units]
    q = [t.astype(F32) for t in q_bf]
    k = [t.astype(F32) for t in k_bf]
    v = [head_cols(2 * DA, s, h).astype(F32) for s, h in units]
    beta = [gates[s][:, h:h + 1] for s, h in units]
    g_col = [g_cum[s][:, HA + h:HA + h + 1] for s, h in units]
    g_row = [g_rows[s][h:h + 1, :] for s, h in units]
    g_last = [g_cum[s][c - 1:c, HA + h:HA + h + 1] for s, h in units]
    decay = [jnp.where(incl, jnp.exp(jnp.where(incl, g_col[u] - g_row[u], 0.0)), 0.0) for u in n]
    exp_g = [jnp.exp(g) for g in g_col]
    kb = [k[u] * beta[u] for u in n]
    lower = [jnp.where(strict, _dot_nt(kb[u].astype(BF16), k_bf[u]) * decay[u], 0.0) for u in n]
    qk = [jnp.where(incl, _dot_nt(q_bf[u], k_bf[u]) * decay[u], 0.0).astype(BF16) for u in n]
    t_inv = _unit_lower_inverses(lower, ri, ci)
    rhs = [jnp.concatenate([v[u] * beta[u], kb[u] * exp_g[u]], axis=1).astype(BF16) for u in n]
    sol = [_dot(t_inv[u].astype(BF16), rhs[u]) for u in n]
    q_dec = [(q[u] * exp_g[u]).astype(BF16) for u in n]
    k_dec = [(k[u] * jnp.exp(g_last[u] - g_col[u])).astype(BF16) for u in n]

    state = [state_ref[u] for u in n]
    state_bf = [s.astype(BF16) for s in state]
    v_new = [(sol[u][:, :HEAD_DIM] - _dot(sol[u][:, HEAD_DIM:].astype(BF16), state_bf[u])).astype(BF16)
             for u in n]
    o = [_dot(q_dec[u], state_bf[u]) + _dot(qk[u], v_new[u]) for u in n]
    for u in n:
        state_ref[u] = state[u] * jnp.exp(g_last[u]) + _dot_tn(k_dec[u], v_new[u])

    onorm = onorm_ref[...]
    for u, (s, h) in enumerate(units):
        sl = slice(h * HEAD_DIM, (h + 1) * HEAD_DIM)
        o_ref[s, :, sl] = (_rms(o[u], onorm) * zs_ref[s, :, sl].astype(F32)).astype(o_ref.dtype)


def _deltanet(qkv, zs, gates, out_norm):
    b, lp, _ = qkv.shape
    c = DN_CHUNK
    nseq = DN_SEQS_PER_STEP
    return pl.pallas_call(
        _deltanet_kernel,
        grid=(b // nseq, lp // c),
        in_specs=[pl.BlockSpec((nseq, c, 3 * DA), lambda bi, i: (bi, i, 0)),
                  pl.BlockSpec((nseq, c, DA), lambda bi, i: (bi, i, 0)),
                  pl.BlockSpec((nseq, c, LANE), lambda bi, i: (bi, i, 0)),
                  _const_spec((1, HEAD_DIM))],
        out_specs=pl.BlockSpec((nseq, c, DA), lambda bi, i: (bi, i, 0)),
        out_shape=jax.ShapeDtypeStruct((b, lp, DA), BF16),
        scratch_shapes=[pltpu.VMEM((nseq * HA, HEAD_DIM, HEAD_DIM), F32)],
        compiler_params=_params("parallel", "arbitrary"),
        name="deltanet",
    )(qkv, zs, gates, out_norm)


def _stickbreak_visit(q_ref, k_ref, v_ref, heads, suffix, first, count, laters, diag_mask):
    tk = suffix.shape[0]
    n_heads = range(len(heads))
    chains = [(b, h) for b in range(count) for h in n_heads]
    index = [first - b for b in range(count)]
    starts = [pl.multiple_of(jnp.maximum(j, 0) * tk, tk) for j in index]

    def masked(c, t):
        b, _ = chains[c]
        if diag_mask is None:
            return t
        return jnp.where(diag_mask if b == 0 else index[b] >= 0, t, 0.0)

    n = range(len(chains))
    qs = [q_ref[0, :, sl] for sl in heads]
    zq = [_dot_nt(qs[h], k_ref[0, pl.ds(starts[b], tk), heads[h]]) * (HEAD_DIM ** -0.5)
          for b, h in chains]
    log_beta = [jnp.minimum(z, 0.0) - jnp.log(1.0 + jnp.exp(-jnp.abs(z))) for z in zq]
    log_1m = [masked(c, log_beta[c] - zq[c]) for c in n]
    hi = [t.astype(BF16) for t in log_1m]
    lo = [(log_1m[c] - hi[c].astype(F32)).astype(BF16) for c in n]
    suf = [_dot(hi[c], suffix) + _dot(lo[c], suffix) for c in n]
    entering, running = {}, list(laters)
    for c, (b, h) in enumerate(chains):
        entering[c] = running[h]
        running[h] = running[h] + suf[c][:, 0:1]
    wgt = [masked(c, jnp.exp(log_beta[c] + (suf[c] - log_1m[c]) + entering[c])).astype(BF16) for c in n]
    pv = [_dot(wgt[c], v_ref[0, pl.ds(starts[b], tk), heads[h]]) for c, (b, h) in enumerate(chains)]
    contrib = tuple(functools.reduce(lambda x, y: x + y, [pv[c] for c in n if chains[c][1] == h])
                    for h in n_heads)
    largest = jnp.max(functools.reduce(jnp.maximum, running))
    return contrib, tuple(running), largest


def _stickbreak_kernel(q_ref, k_ref, v_ref, o_ref):
    tq = q_ref.shape[1]
    tk = tq
    i = pl.program_id(1)
    heads = [slice(h * HEAD_DIM, (h + 1) * HEAD_DIM) for h in range(HB)]
    suffix = jnp.where(lax.broadcasted_iota(jnp.int32, (tk, tk), 0)
                       >= lax.broadcasted_iota(jnp.int32, (tk, tk), 1), 1.0, 0.0).astype(BF16)
    strictly_causal = (lax.broadcasted_iota(jnp.int32, (tq, tk), 1)
                       < lax.broadcasted_iota(jnp.int32, (tq, tk), 0))
    accs, laters, largest = _stickbreak_visit(
        q_ref, k_ref, v_ref, heads, suffix, i, ATT_FIRST_BLOCKS,
        (jnp.zeros((tq, 1), F32),) * HB, strictly_causal)

    def more(carry):
        j, largest, _, _ = carry
        return jnp.logical_and(j >= 0, largest > EXP_UNDERFLOW)

    def body(carry):
        j, _, accs, laters = carry
        contrib, laters, largest = _stickbreak_visit(q_ref, k_ref, v_ref, heads, suffix, j, 1,
                                                     laters, None)
        return j - 1, largest, tuple(a + c for a, c in zip(accs, contrib)), laters

    _, _, accs, _ = lax.while_loop(more, body, (i - ATT_FIRST_BLOCKS, largest, accs, laters))
    for sl, acc in zip(heads, accs):
        o_ref[0, :, sl] = acc.astype(o_ref.dtype)


def _stickbreak(qkv):
    b, lp, _ = qkv.shape
    tq = ATT_BLOCK
    return pl.pallas_call(
        _stickbreak_kernel,
        grid=(b, lp // tq),
        in_specs=[pl.BlockSpec((1, tq, DB), lambda bi, i: (bi, i, 0)),
                  pl.BlockSpec((1, lp, DB), lambda bi, i: (bi, 0, 1)),
                  pl.BlockSpec((1, lp, DB), lambda bi, i: (bi, 0, 2))],
        out_specs=pl.BlockSpec((1, tq, DB), lambda bi, i: (bi, i, 0)),
        out_shape=jax.ShapeDtypeStruct((b, lp, DB), BF16),
        compiler_params=_params("parallel", "arbitrary"),
        name="stickbreak",
    )(qkv, qkv, qkv)


def _row_subtiles(rows):
    return [slice(r, r + FFN_SUBTILE) for r in range(0, rows, FFN_SUBTILE)]


def _residual_ffn(ms, subs, h_ref, npost_ref, nfpre_ref, wg_ref, wu_ref, wd_ref, nfpost_ref, out_ref):
    h1 = [h_ref[r, :] + _rms(m, npost_ref[...]) for m, r in zip(ms, subs)]
    hn = [_rms(t, nfpre_ref[...]).astype(BF16) for t in h1]
    g = [_dot(t, wg_ref[...]) for t in hn]
    u = [_dot(t, wu_ref[...]) for t in hn]
    a = [(gi * _sigmoid(gi) * ui).astype(BF16) for gi, ui in zip(g, u)]
    f = [_dot(t, wd_ref[...]) for t in a]
    for r, h1i, fi in zip(subs, h1, f):
        out_ref[r, :] = h1i + _rms(fi, nfpost_ref[...])


def _hyb_out_ffn_kernel(oa_ref, ob_ref, wo_ref, *rest):
    subs = _row_subtiles(oa_ref.shape[0])
    ms = [_dot(oa_ref[r, :], wo_ref[:DA, :]) + _dot(ob_ref[r, :], wo_ref[DA:, :]) for r in subs]
    _residual_ffn(ms, subs, *rest)


def _glu_out_ffn_kernel(y_ref, wglu_ref, *rest):
    d = wglu_ref.shape[1] // 2
    subs = _row_subtiles(y_ref.shape[0])
    yv = [_dot(y_ref[r, :], wglu_ref[...]) for r in subs]
    ms = [t[:, :d] * _sigmoid(t[:, d:]) for t in yv]
    _residual_ffn(ms, subs, *rest)


def _mixer_out_ffn(kernel_fn, acts, w_mix, h, npost, nfpre, wg, wu, wd, nfpost, keep=None):
    b, lp, d = h.shape
    dff = wg.shape[1]
    tm = 2 * FFN_SUBTILE
    weights = [_const_spec(w_mix.shape)]
    tail = [_const_spec((1, d)), _const_spec((1, d)), _const_spec((d, dff)), _const_spec((d, dff)),
            _const_spec((dff, d)), _const_spec((1, d))]
    if keep is None:
        rows = b * lp
        acts = [a.reshape(rows, a.shape[2]) for a in acts]
        h = h.reshape(rows, d)
        grid = (rows // tm,)
        row = lambda width: pl.BlockSpec((tm, width), lambda i: (i, 0))
        out_spec, out_shape, sem = row(d), (rows, d), ("parallel",)
    else:
        first, n_rows = keep
        assert first % N_META == 0 and tm % N_META == 0
        grid = (b, n_rows // tm)
        row = lambda width: pl.BlockSpec((None, pl.Element(tm), pl.Element(width)),
                                         lambda bi, i: (bi, pl.multiple_of(first + i * tm, N_META), 0))
        out_spec = pl.BlockSpec((None, tm, d), lambda bi, i: (bi, i, 0))
        out_shape, sem = (b, n_rows, d), ("parallel", "parallel")
    assert grid[-1] * tm == (b * lp if keep is None else keep[1])
    return pl.pallas_call(
        kernel_fn,
        grid=grid,
        in_specs=[row(a.shape[-1]) for a in acts] + weights + [row(d)] + tail,
        out_specs=out_spec,
        out_shape=jax.ShapeDtypeStruct(out_shape, F32),
        compiler_params=_params(*sem),
        name="mixer_out_ffn",
    )(*acts, w_mix, h, npost.reshape(1, d), nfpre.reshape(1, d), wg, wu, wd, nfpost.reshape(1, d))


def _s5_kernel(hs_ref, bmat_ref, cmat_ref, a_ref, d_ref, y_ref, state_ref, bu_ref):
    tl, bsz, width = hs_ref.shape
    rows = tl * bsz
    n_tiles = width // LANE

    @pl.when(pl.program_id(0) == 0)
    def _():
        state_ref[...] = jnp.zeros_like(state_ref)

    def channel_tile(k):
        return hs_ref[:, :, k * LANE:(k + 1) * LANE].reshape(rows, LANE)

    def input_map(k):
        bu_ref[k % 2] = _dot(channel_tile(k).astype(BF16), bmat_ref[k])

    input_map(0)
    for k in range(n_tiles):
        if k + 1 < n_tiles:
            input_map(k + 1)
        buf = bu_ref.at[k % 2]
        a_re = jnp.broadcast_to(a_ref[k, 0:1, :], (bsz, S5_HALF))
        a_im = jnp.broadcast_to(a_ref[k, 1:2, :], (bsz, S5_HALF))
        st = state_ref[k]
        x_re, x_im = st[:, :S5_HALF], st[:, S5_HALF:]
        for t in range(tl):
            rs = slice(t * bsz, (t + 1) * bsz)
            bu = buf[rs, :]
            x_re, x_im = (a_re * x_re - a_im * x_im + bu[:, :S5_HALF],
                          a_re * x_im + a_im * x_re + bu[:, S5_HALF:])
            buf[rs, :] = jnp.concatenate([x_re, x_im], axis=1)
        state_ref[k] = jnp.concatenate([x_re, x_im], axis=1)

        lanes = slice(k * LANE, (k + 1) * LANE)
        y = _dot(buf[...].astype(BF16), cmat_ref[k]) + d_ref[:, lanes] * channel_tile(k)
        y_ref[:, :, lanes] = jax.nn.gelu(y).reshape(tl, bsz, LANE).astype(y_ref.dtype)


def _s5_scan(hs_t, bmat, cmat, a_bar, d_skip):
    lp, bsz, width = hs_t.shape
    tl = S5_TILE
    n_tiles = width // LANE
    return pl.pallas_call(
        _s5_kernel,
        grid=(lp // tl,),
        in_specs=[pl.BlockSpec((tl, bsz, width), lambda i: (i, 0, 0)),
                  _const_spec(bmat.shape), _const_spec(cmat.shape),
                  _const_spec(a_bar.shape), _const_spec((1, width))],
        out_specs=pl.BlockSpec((tl, bsz, width), lambda i: (i, 0, 0)),
        out_shape=jax.ShapeDtypeStruct((lp, bsz, width), BF16),
        scratch_shapes=[pltpu.VMEM((n_tiles, bsz, 2 * S5_HALF), F32),
                        pltpu.VMEM((2, tl * bsz, 2 * S5_HALF), F32)],
        compiler_params=_params("arbitrary"),
        name="s5_scan",
    )(hs_t, bmat, cmat, a_bar, d_skip.reshape(1, width))


def _s5_discretise_kernel(lre_ref, lim_ref, step_ref, bre_ref, bim_ref,
                          are_ref, aim_ref, bbre_ref, bbim_ref):
    lam_re, lam_im = lre_ref[...], lim_ref[...]
    dt = jnp.exp(step_ref[...])
    mag = jnp.exp(lam_re * dt)
    a_re = mag * jnp.cos(lam_im * dt)
    a_im = mag * jnp.sin(lam_im * dt)
    den = lam_re * lam_re + lam_im * lam_im
    f_re = ((a_re - 1.0) * lam_re + a_im * lam_im) / den
    f_im = (a_im * lam_re - (a_re - 1.0) * lam_im) / den
    are_ref[...] = a_re
    aim_ref[...] = a_im
    bbre_ref[...] = f_re * bre_ref[...] - f_im * bim_ref[...]
    bbim_ref[...] = f_re * bim_ref[...] + f_im * bre_ref[...]


def _s5_discretise(lam_re, lam_im, log_step, b_re, b_im, c_re, c_im):
    n_groups = lam_re.shape[0]
    n_tiles = n_groups // GROUPS_PER_LANE_TILE
    flat = (n_groups, STATE * GROUP)
    spread = lambda t: jnp.repeat(t, GROUP, axis=1)
    a_re, a_im, bb_re, bb_im = pl.pallas_call(
        _s5_discretise_kernel,
        out_shape=[jax.ShapeDtypeStruct(flat, F32)] * 4,
        name="s5_discretise",
    )(spread(lam_re), spread(lam_im), log_step.reshape(n_groups, 1),
      b_re.reshape(flat), b_im.reshape(flat))
    a_re, a_im = a_re[:, ::GROUP], a_im[:, ::GROUP]
    bb_re = bb_re.reshape(n_groups, STATE, GROUP)
    bb_im = bb_im.reshape(n_groups, STATE, GROUP)
    eye = jnp.eye(GROUPS_PER_LANE_TILE, dtype=F32)

    def in_map(bb):
        t = bb.reshape(n_tiles, GROUPS_PER_LANE_TILE, STATE, GROUP)
        t = jnp.einsum("kgpc,gh->kgchp", t, eye)
        return t.reshape(n_tiles, LANE, S5_HALF)

    def out_map(cc):
        t = cc.reshape(n_tiles, GROUPS_PER_LANE_TILE, GROUP, STATE)
        t = jnp.einsum("kgcp,gh->kgphc", t, eye)
        return t.reshape(n_tiles, S5_HALF, LANE)

    bmat = jnp.concatenate([in_map(bb_re), in_map(bb_im)], axis=2).astype(BF16)
    cmat = jnp.concatenate([out_map(c_re), -out_map(c_im)], axis=1).astype(BF16)
    a_bar = jnp.stack([a_re.reshape(n_tiles, S5_HALF), a_im.reshape(n_tiles, S5_HALF)], axis=1)
    return bmat, cmat, a_bar


def kernel(x, meta_tokens, norm_mix_pre, norm_mix_post, norm_ffn_pre, norm_ffn_post, ffn_w_gate,
           ffn_w_up, ffn_w_down, hyb_w_in, hyb_conv_w, dn_a_log, dn_dt_bias, dn_out_norm, hyb_w_out,
           ssm_w_in, ssm_lambda_re, ssm_lambda_im, ssm_log_step, ssm_b_re, ssm_b_im, ssm_c_re,
           ssm_c_im, ssm_d, ssm_w_glu):
    bsz, seq, d = x.shape
    l_real = N_META + seq
    lp = -(-l_real // SEQ_ALIGN) * SEQ_ALIGN
    w_in = hyb_w_in[0]
    n_gate = 2 * HA
    w_cat = jnp.concatenate([
        w_in[:, :4 * DA],
        jnp.pad(w_in[:, 4 * DA:4 * DA + n_gate], ((0, 0), (0, LANE - n_gate))),
        w_in[:, 4 * DA + n_gate:]], axis=1).astype(BF16)
    gate_par = jnp.zeros((2, LANE), F32)
    gate_par = gate_par.at[0, HA:2 * HA].set(dn_a_log[0].astype(F32))
    gate_par = gate_par.at[1, HA:2 * HA].set(dn_dt_bias[0].astype(F32))
    h, qkv_a, zs_a, gates, qkv_b = _hyb_in(x, meta_tokens.astype(x.dtype), lp, norm_mix_pre[0], w_cat,
                                           hyb_conv_w[0].astype(F32), gate_par)
    o_a = _deltanet(qkv_a, zs_a, gates, dn_out_norm[0].reshape(1, HEAD_DIM).astype(F32))
    o_b = _stickbreak(qkv_b)
    h = _mixer_out_ffn(_hyb_out_ffn_kernel, [o_a, o_b], hyb_w_out[0].astype(BF16), h,
                       norm_mix_post[0], norm_ffn_pre[0], ffn_w_gate[0].astype(BF16),
                       ffn_w_up[0].astype(BF16), ffn_w_down[0].astype(BF16), norm_ffn_post[0])

    width = ssm_w_in.shape[2]
    (hs,) = _norm_proj(h, norm_mix_pre[1], ssm_w_in[0].astype(BF16), ((0, width),), (F32,))
    bmat, cmat, a_bar = _s5_discretise(ssm_lambda_re[0].astype(F32), ssm_lambda_im[0].astype(F32),
                                       ssm_log_step[0].astype(F32), ssm_b_re[0].astype(F32),
                                       ssm_b_im[0].astype(F32), ssm_c_re[0].astype(F32),
                                       ssm_c_im[0].astype(F32))
    hs_t = jnp.transpose(hs.reshape(bsz, lp, width), (1, 0, 2))
    y_t = _s5_scan(hs_t, bmat, cmat, a_bar, ssm_d[0].astype(F32))
    y = jnp.transpose(y_t, (1, 0, 2))
    return _mixer_out_ffn(_glu_out_ffn_kernel, [y], ssm_w_glu[0].astype(BF16),
                          h.reshape(bsz, lp, d), norm_mix_post[1], norm_ffn_pre[1],
                          ffn_w_gate[1].astype(BF16), ffn_w_up[1].astype(BF16),
                          ffn_w_down[1].astype(BF16), norm_ffn_post[1], keep=(N_META, seq))
```

```python
import functools

import jax
import jax.numpy as jnp
from jax import lax
from jax.experimental import pallas as pl
from jax.experimental.pallas import tpu as pltpu

N_META = 16
HEAD_DIM = 128
HA = 4
HB = 4
DA = HA * HEAD_DIM
DB = HB * HEAD_DIM
CONV_WIDTH = 4
GROUP = 16
STATE = 64
EPS = 1e-6

LANE = 128
SEQ_ALIGN = 128
DN_CHUNK = 128
DN_SEQS_PER_STEP = 4
INV_BASE = 16
ATT_BLOCK = 128
ATT_FIRST_BLOCKS = 3
FFN_SUBTILE = 256
S5_TILE = 64
S5_PITCH = S5_TILE + 8
GROUPS_PER_LANE_TILE = LANE // GROUP
S5_HALF = GROUPS_PER_LANE_TILE * STATE
VMEM_LIMIT = 56 * 1024 * 1024
EXP_UNDERFLOW = -105.0

F32 = jnp.float32
BF16 = jnp.bfloat16
HIGHEST = lax.Precision.HIGHEST


def _dot(a, b):
    return jnp.dot(a, b, preferred_element_type=F32)


def _dot_nt(a, b):
    return lax.dot_general(a, b, (((1,), (1,)), ((), ())), preferred_element_type=F32)


def _dot_tn(a, b):
    return lax.dot_general(a, b, (((0,), (0,)), ((), ())), preferred_element_type=F32)


def _dot_f32(a, b):
    return jnp.dot(a, b, preferred_element_type=F32, precision=HIGHEST)


def _rms(x, w):
    return x * lax.rsqrt(jnp.mean(x * x, axis=-1, keepdims=True) + EPS) * w


def _sigmoid(x):
    return 1.0 / (1.0 + jnp.exp(-x))


def _softplus(x):
    return jnp.maximum(x, 0.0) + jnp.log(1.0 + jnp.exp(-jnp.abs(x)))


def _const_spec(shape):
    zeros = (0,) * len(shape)
    return pl.BlockSpec(shape, lambda *_: zeros, pipeline_mode=pl.Buffered(1))


def _params(*sem):
    return pltpu.CompilerParams(dimension_semantics=sem, vmem_limit_bytes=VMEM_LIMIT)


def _norm_proj_kernel(x_ref, nw_ref, w_ref, *out_refs, col_splits):
    y = _rms(x_ref[...], nw_ref[...]).astype(BF16)
    for out_ref, (start, width) in zip(out_refs, col_splits):
        out_ref[...] = _dot(y, w_ref[:, start:start + width]).astype(out_ref.dtype)


def _norm_proj(x, norm_w, w, col_splits, out_dtypes, tm=512):
    t, d = x.shape
    n = w.shape[1]
    return pl.pallas_call(
        functools.partial(_norm_proj_kernel, col_splits=col_splits),
        grid=(t // tm,),
        in_specs=[pl.BlockSpec((tm, d), lambda i: (i, 0)),
                  _const_spec((1, d)),
                  _const_spec((d, n))],
        out_specs=[pl.BlockSpec((tm, width), lambda i: (i, 0)) for _, width in col_splits],
        out_shape=[jax.ShapeDtypeStruct((t, width), dt)
                   for (_, width), dt in zip(col_splits, out_dtypes)],
        compiler_params=_params("parallel"),
        name="norm_proj",
    )(x, norm_w.reshape(1, d), w)


HYB_COLS = (("qkv_a", 0, 3 * DA), ("z_a", 3 * DA, DA), ("gates", 4 * DA, LANE),
            ("qkv_b", 4 * DA + LANE, 3 * DB))
HALO = 8


def _sequence_tile(x_ref, meta_ref, tile, n_tiles, n_pad):
    xb = x_ref[...]
    tm = xb.shape[0]
    first = jnp.concatenate([meta_ref[...], xb[:tm - N_META]], axis=0)
    if n_pad:
        last = jnp.concatenate([xb[n_pad:], jnp.zeros((n_pad, xb.shape[1]), xb.dtype)], axis=0)
    else:
        last = xb
    return jnp.where(tile == 0, first, jnp.where(tile == n_tiles - 1, last, xb))


def _hyb_in_kernel(x_ref, meta_ref, nw_ref, w_ref, convw_ref, gpar_ref,
                   h_ref, qkv_ref, zs_ref, gate_ref, qkvb_ref, tail_ref, *, n_pad):
    cols = {name: slice(start, start + width) for name, start, width in HYB_COLS}

    @pl.when(pl.program_id(1) == 0)
    def _():
        tail_ref[...] = jnp.zeros_like(tail_ref)

    h0 = _sequence_tile(x_ref, meta_ref, pl.program_id(1), pl.num_programs(1), n_pad)
    h_ref[0] = h0
    y = _rms(h0, nw_ref[...]).astype(BF16)
    first_rows = lax.broadcasted_iota(jnp.int32, (HALO, DA), 0)

    def conv_silu(pre, cs):
        tail = tail_ref[:, cs]
        acc = pre * convw_ref[CONV_WIDTH - 1:CONV_WIDTH, cs]
        for s in range(1, CONV_WIDTH):
            rolled = pltpu.roll(pre, s, 0)
            head = jnp.where(first_rows < s, pltpu.roll(tail, s, 0), rolled[:HALO])
            shifted = jnp.concatenate([head, rolled[HALO:]], axis=0)
            acc = acc + shifted * convw_ref[CONV_WIDTH - 1 - s:CONV_WIDTH - s, cs]
        tail_ref[:, cs] = pre[pre.shape[0] - HALO:, :]
        return acc * _sigmoid(acc)

    def unit_heads(xc, base, gain):
        for h in range(HA):
            t = xc[:, h * HEAD_DIM:(h + 1) * HEAD_DIM]
            t = t * (lax.rsqrt(jnp.sum(t * t, axis=-1, keepdims=True) + EPS) * gain)
            qkv_ref[0, :, base + h * HEAD_DIM:base + (h + 1) * HEAD_DIM] = t.astype(qkv_ref.dtype)

    def post_q(pre):
        unit_heads(conv_silu(pre, slice(0, DA)), 0, HEAD_DIM ** -0.5)

    def post_k(pre):
        unit_heads(conv_silu(pre, slice(DA, 2 * DA)), DA, 1.0)

    def post_v(pre):
        qkv_ref[0, :, 2 * DA:] = conv_silu(pre, slice(2 * DA, 3 * DA)).astype(qkv_ref.dtype)

    def post_z(zlin):
        zs_ref[0] = (zlin * _sigmoid(zlin)).astype(zs_ref.dtype)

    def post_gates(ba):
        g = -jnp.exp(gpar_ref[0:1, :]) * _softplus(ba + gpar_ref[1:2, :])
        gate_ref[0] = jnp.where(lax.broadcasted_iota(jnp.int32, ba.shape, 1) < HA, _sigmoid(ba), g)

    def post_b(part):
        def store(pre):
            qkvb_ref[0, :, part * DB:(part + 1) * DB] = pre.astype(qkvb_ref.dtype)
        return store

    qa, qb = cols["qkv_a"].start, cols["qkv_b"].start
    stages = [(slice(qa, qa + DA), post_q), (slice(qa + DA, qa + 2 * DA), post_k),
              (slice(qa + 2 * DA, qa + 3 * DA), post_v), (cols["z_a"], post_z),
              (cols["gates"], post_gates)]
    stages += [(slice(qb + part * DB, qb + (part + 1) * DB), post_b(part)) for part in range(3)]
    pending = _dot(y, w_ref[:, stages[0][0]])
    for idx, (_, post) in enumerate(stages):
        upcoming = _dot(y, w_ref[:, stages[idx + 1][0]]) if idx + 1 < len(stages) else None
        post(pending)
        pending = upcoming


def _hyb_in(x, meta, lp, norm_w, w, conv_w, gate_par):
    b, seq, d = x.shape
    n_pad = lp - (N_META + seq)
    tm = next(rows for rows in (384, 256, 128) if lp % rows == 0)
    assert lp // tm >= 2 and seq >= tm and n_pad % 8 == 0 and n_pad <= tm
    row = lambda width: pl.BlockSpec((1, tm, width), lambda bi, i: (bi, i, 0))
    n_tiles = lp // tm

    def x_rows(bi, i):
        start = (i * tm - N_META + N_META * (i == 0).astype(jnp.int32)
                 - n_pad * (i == n_tiles - 1).astype(jnp.int32))
        return bi, pl.multiple_of(start, HALO), 0

    x_spec = pl.BlockSpec((None, pl.Element(tm), pl.Element(d)), x_rows)
    widths = {name: width for name, _, width in HYB_COLS}
    return pl.pallas_call(
        functools.partial(_hyb_in_kernel, n_pad=n_pad),
        grid=(b, lp // tm),
        in_specs=[x_spec, _const_spec((N_META, d)), _const_spec((1, d)), _const_spec(w.shape),
                  _const_spec(conv_w.shape), _const_spec(gate_par.shape)],
        out_specs=[row(d), row(widths["qkv_a"]), row(widths["z_a"]), row(widths["gates"]),
                   row(widths["qkv_b"])],
        out_shape=[jax.ShapeDtypeStruct((b, lp, d), F32),
                   jax.ShapeDtypeStruct((b, lp, widths["qkv_a"]), BF16),
                   jax.ShapeDtypeStruct((b, lp, widths["z_a"]), BF16),
                   jax.ShapeDtypeStruct((b, lp, widths["gates"]), F32),
                   jax.ShapeDtypeStruct((b, lp, widths["qkv_b"]), BF16)],
        scratch_shapes=[pltpu.VMEM((HALO, widths["qkv_a"]), F32)],
        compiler_params=_params("parallel", "arbitrary"),
        name="hyb_in",
    )(x, meta, norm_w.reshape(1, d), w, conv_w, gate_par)


def _unit_lower_inverses(lowers, ri, ci):
    c = lowers[0].shape[0]

    def same_block(size):
        shift = size.bit_length() - 1
        return (ri >> shift) == (ci >> shift)

    def mm(a, b):
        return _dot(a.astype(BF16), b.astype(BF16))

    eye = jnp.where(ci == ri, 1.0, 0.0).astype(F32)
    in_base = same_block(INV_BASE)
    xs = [jnp.where(in_base, -low, 0.0) for low in lowers]
    ps = [eye + x for x in xs]
    power = 2
    while power < INV_BASE:
        xs = [mm(x, x) for x in xs]
        ps = [p + mm(p, x) for p, x in zip(ps, xs)]
        power *= 2
    size = INV_BASE
    while size < c:
        joins = jnp.logical_and(same_block(2 * size), jnp.logical_not(same_block(size)))
        ys = [mm(jnp.where(joins, low, 0.0), p) for low, p in zip(lowers, ps)]
        ps = [p - mm(p, y) for p, y in zip(ps, ys)]
        size *= 2
    return ps


def _deltanet_kernel(qkv_ref, zs_ref, gate_ref, onorm_ref, o_ref, state_ref):
    nseq, c = qkv_ref.shape[0], qkv_ref.shape[1]
    units = [(s, h) for s in range(nseq) for h in range(HA)]
    n = range(len(units))

    @pl.when(pl.program_id(1) == 0)
    def _():
        state_ref[...] = jnp.zeros_like(state_ref)

    ri = lax.broadcasted_iota(jnp.int32, (c, c), 0)
    ci = lax.broadcasted_iota(jnp.int32, (c, c), 1)
    incl = ci <= ri
    strict = ci < ri
    tril = jnp.where(incl, 1.0, 0.0).astype(F32)
    sel = jnp.where(lax.broadcasted_iota(jnp.int32, (8, LANE), 1)
                    == lax.broadcasted_iota(jnp.int32, (8, LANE), 0) + HA, 1.0, 0.0).astype(F32)
    gates = [gate_ref[s] for s in range(nseq)]
    g_cum = [_dot_f32(tril, g) for g in gates]
    g_rows = [lax.dot_general(sel, g, (((1,), (1,)), ((), ())), preferred_element_type=F32,
                              precision=HIGHEST) for g in g_cum]

    def head_cols(base, s, h):
        return qkv_ref[s, :, base + h * HEAD_DIM:base + (h + 1) * HEAD_DIM]

    q_bf = [head_cols(0, s, h) for s, h in units]
    k_bf = [head_cols(DA, s, h) for s, h in units]
    q = [t.astype(F32) for t in q_bf]
    k = [t.astype(F32) for t in k_bf]
    v = [head_cols(2 * DA, s, h).astype(F32) for s, h in units]
    beta = [gates[s][:, h:h + 1] for s, h in units]
    g_col = [g_cum[s][:, HA + h:HA + h + 1] for s, h in units]
    g_row = [g_rows[s][h:h + 1, :] for s, h in units]
    g_last = [g_cum[s][c - 1:c, HA + h:HA + h + 1] for s, h in units]
    decay = [jnp.where(incl, jnp.exp(jnp.where(incl, g_col[u] - g_row[u], 0.0)), 0.0) for u in n]
    exp_g = [jnp.exp(g) for g in g_col]
    kb = [k[u] * beta[u] for u in n]
    lower = [jnp.where(strict, _dot_nt(kb[u].astype(BF16), k_bf[u]) * decay[u], 0.0) for u in n]
    qk = [jnp.where(incl, _dot_nt(q_bf[u], k_bf[u]) * decay[u], 0.0).astype(BF16) for u in n]
    t_inv = _unit_lower_inverses(lower, ri, ci)
    rhs = [jnp.concatenate([v[u] * beta[u], kb[u] * exp_g[u]], axis=1).astype(BF16) for u in n]
    sol = [_dot(t_inv[u].astype(BF16), rhs[u]) for u in n]
    q_dec = [(q[u] * exp_g[u]).astype(BF16) for u in n]
    k_dec = [(k[u] * jnp.exp(g_last[u] - g_col[u])).astype(BF16) for u in n]

    state = [state_ref[u] for u in n]
    state_bf = [s.astype(BF16) for s in state]
    v_new = [(sol[u][:, :HEAD_DIM] - _dot(sol[u][:, HEAD_DIM:].astype(BF16), state_bf[u])).astype(BF16)
             for u in n]
    o = [_dot(q_dec[u], state_bf[u]) + _dot(qk[u], v_new[u]) for u in n]
    for u in n:
        state_ref[u] = state[u] * jnp.exp(g_last[u]) + _dot_tn(k_dec[u], v_new[u])

    onorm = onorm_ref[...]
    for u, (s, h) in enumerate(units):
        sl = slice(h * HEAD_DIM, (h + 1) * HEAD_DIM)
        o_ref[s, :, sl] = (_rms(o[u], onorm) * zs_ref[s, :, sl].astype(F32)).astype(o_ref.dtype)


def _deltanet(qkv, zs, gates, out_norm):
    b, lp, _ = qkv.shape
    c = DN_CHUNK
    nseq = DN_SEQS_PER_STEP
    return pl.pallas_call(
        _deltanet_kernel,
        grid=(b // nseq, lp // c),
        in_specs=[pl.BlockSpec((nseq, c, 3 * DA), lambda bi, i: (bi, i, 0)),
                  pl.BlockSpec((nseq, c, DA), lambda bi, i: (bi, i, 0)),
                  pl.BlockSpec((nseq, c, LANE), lambda bi, i: (bi, i, 0)),
                  _const_spec((1, HEAD_DIM))],
        out_specs=pl.BlockSpec((nseq, c, DA), lambda bi, i: (bi, i, 0)),
        out_shape=jax.ShapeDtypeStruct((b, lp, DA), BF16),
        scratch_shapes=[pltpu.VMEM((nseq * HA, HEAD_DIM, HEAD_DIM), F32)],
        compiler_params=_params("parallel", "arbitrary"),
        name="deltanet",
    )(qkv, zs, gates, out_norm)


def _stickbreak_visit(q_ref, k_ref, v_ref, heads, suffix, first, count, laters, diag_mask):
    tk = suffix.shape[0]
    n_heads = range(len(heads))
    chains = [(b, h) for b in range(count) for h in n_heads]
    index = [first - b for b in range(count)]
    starts = [pl.multiple_of(jnp.maximum(j, 0) * tk, tk) for j in index]

    def masked(c, t):
        b, _ = chains[c]
        if diag_mask is None:
            return t
        return jnp.where(diag_mask if b == 0 else index[b] >= 0, t, 0.0)

    n = range(len(chains))
    qs = [q_ref[0, :, sl] for sl in heads]
    zq = [_dot_nt(qs[h], k_ref[0, pl.ds(starts[b], tk), heads[h]]) * (HEAD_DIM ** -0.5)
          for b, h in chains]
    log_beta = [jnp.minimum(z, 0.0) - jnp.log(1.0 + jnp.exp(-jnp.abs(z))) for z in zq]
    log_1m = [masked(c, log_beta[c] - zq[c]) for c in n]
    hi = [t.astype(BF16) for t in log_1m]
    lo = [(log_1m[c] - hi[c].astype(F32)).astype(BF16) for c in n]
    suf = [_dot(hi[c], suffix) + _dot(lo[c], suffix) for c in n]
    entering, running = {}, list(laters)
    for c, (b, h) in enumerate(chains):
        entering[c] = running[h]
        running[h] = running[h] + suf[c][:, 0:1]
    wgt = [masked(c, jnp.exp(log_beta[c] + (suf[c] - log_1m[c]) + entering[c])).astype(BF16) for c in n]
    pv = [_dot(wgt[c], v_ref[0, pl.ds(starts[b], tk), heads[h]]) for c, (b, h) in enumerate(chains)]
    contrib = tuple(functools.reduce(lambda x, y: x + y, [pv[c] for c in n if chains[c][1] == h])
                    for h in n_heads)
    largest = jnp.max(functools.reduce(jnp.maximum, running))
    return contrib, tuple(running), largest


def _stickbreak_kernel(q_ref, k_ref, v_ref, o_ref):
    tq = q_ref.shape[1]
    tk = tq
    i = pl.program_id(1)
    heads = [slice(h * HEAD_DIM, (h + 1) * HEAD_DIM) for h in range(HB)]
    suffix = jnp.where(lax.broadcasted_iota(jnp.int32, (tk, tk), 0)
                       >= lax.broadcasted_iota(jnp.int32, (tk, tk), 1), 1.0, 0.0).astype(BF16)
    strictly_causal = (lax.broadcasted_iota(jnp.int32, (tq, tk), 1)
                       < lax.broadcasted_iota(jnp.int32, (tq, tk), 0))
    accs, laters, largest = _stickbreak_visit(
        q_ref, k_ref, v_ref, heads, suffix, i, ATT_FIRST_BLOCKS,
        (jnp.zeros((tq, 1), F32),) * HB, strictly_causal)

    def more(carry):
        j, largest, _, _ = carry
        return jnp.logical_and(j >= 0, largest > EXP_UNDERFLOW)

    def body(carry):
        j, _, accs, laters = carry
        contrib, laters, largest = _stickbreak_visit(q_ref, k_ref, v_ref, heads, suffix, j, 1,
                                                     laters, None)
        return j - 1, largest, tuple(a + c for a, c in zip(accs, contrib)), laters

    _, _, accs, _ = lax.while_loop(more, body, (i - ATT_FIRST_BLOCKS, largest, accs, laters))
    for sl, acc in zip(heads, accs):
        o_ref[0, :, sl] = acc.astype(o_ref.dtype)


def _stickbreak(qkv):
    b, lp, _ = qkv.shape
    tq = ATT_BLOCK
    return pl.pallas_call(
        _stickbreak_kernel,
        grid=(b, lp // tq),
        in_specs=[pl.BlockSpec((1, tq, DB), lambda bi, i: (bi, i, 0)),
                  pl.BlockSpec((1, lp, DB), lambda bi, i: (bi, 0, 1)),
                  pl.BlockSpec((1, lp, DB), lambda bi, i: (bi, 0, 2))],
        out_specs=pl.BlockSpec((1, tq, DB), lambda bi, i: (bi, i, 0)),
        out_shape=jax.ShapeDtypeStruct((b, lp, DB), BF16),
        compiler_params=_params("parallel", "arbitrary"),
        name="stickbreak",
    )(qkv, qkv, qkv)


def _row_subtiles(rows):
    return [slice(r, r + FFN_SUBTILE) for r in range(0, rows, FFN_SUBTILE)]


def _residual_ffn(ms, subs, h_ref, npost_ref, nfpre_ref, wg_ref, wu_ref, wd_ref, nfpost_ref, out_ref):
    h1 = [h_ref[r, :] + _rms(m, npost_ref[...]) for m, r in zip(ms, subs)]
    hn = [_rms(t, nfpre_ref[...]).astype(BF16) for t in h1]
    g = [_dot(t, wg_ref[...]) for t in hn]
    u = [_dot(t, wu_ref[...]) for t in hn]
    a = [(gi * _sigmoid(gi) * ui).astype(BF16) for gi, ui in zip(g, u)]
    f = [_dot(t, wd_ref[...]) for t in a]
    for r, h1i, fi in zip(subs, h1, f):
        out_ref[r, :] = h1i + _rms(fi, nfpost_ref[...])


def _hyb_out_ffn_kernel(oa_ref, ob_ref, wo_ref, *rest):
    subs = _row_subtiles(oa_ref.shape[0])
    ms = [_dot(oa_ref[r, :], wo_ref[:DA, :]) + _dot(ob_ref[r, :], wo_ref[DA:, :]) for r in subs]
    _residual_ffn(ms, subs, *rest)


def _glu_out_ffn_kernel(y_ref, wglu_ref, *rest):
    d = wglu_ref.shape[1] // 2
    subs = _row_subtiles(y_ref.shape[0])
    yv = [_dot(y_ref[r, :], wglu_ref[...]) for r in subs]
    ms = [t[:, :d] * _sigmoid(t[:, d:]) for t in yv]
    _residual_ffn(ms, subs, *rest)


def _mixer_out_ffn(kernel_fn, acts, w_mix, h, npost, nfpre, wg, wu, wd, nfpost, keep=None):
    b, lp, d = h.shape
    dff = wg.shape[1]
    tm = 2 * FFN_SUBTILE
    weights = [_const_spec(w_mix.shape)]
    tail = [_const_spec((1, d)), _const_spec((1, d)), _const_spec((d, dff)), _const_spec((d, dff)),
            _const_spec((dff, d)), _const_spec((1, d))]
    if keep is None:
        rows = b * lp
        acts = [a.reshape(rows, a.shape[2]) for a in acts]
        h = h.reshape(rows, d)
        grid = (rows // tm,)
        row = lambda width: pl.BlockSpec((tm, width), lambda i: (i, 0))
        out_spec, out_shape, sem = row(d), (rows, d), ("parallel",)
    else:
        first, n_rows = keep
        assert first % N_META == 0 and tm % N_META == 0
        grid = (b, n_rows // tm)
        row = lambda width: pl.BlockSpec((None, pl.Element(tm), pl.Element(width)),
                                         lambda bi, i: (bi, pl.multiple_of(first + i * tm, N_META), 0))
        out_spec = pl.BlockSpec((None, tm, d), lambda bi, i: (bi, i, 0))
        out_shape, sem = (b, n_rows, d), ("parallel", "parallel")
    assert grid[-1] * tm == (b * lp if keep is None else keep[1])
    return pl.pallas_call(
        kernel_fn,
        grid=grid,
        in_specs=[row(a.shape[-1]) for a in acts] + weights + [row(d)] + tail,
        out_specs=out_spec,
        out_shape=jax.ShapeDtypeStruct(out_shape, F32),
        compiler_params=_params(*sem),
        name="mixer_out_ffn",
    )(*acts, w_mix, h, npost.reshape(1, d), nfpre.reshape(1, d), wg, wu, wd, nfpost.reshape(1, d))


def _s5_kernel(hs_ref, bmat_ref, cmat_ref, a_ref, d_ref, y_ref, state_ref, bu_ref, u_ref,
               swap_in_ref, swap_out_ref):
    bsz, tl, width = hs_ref.shape
    n_tiles = width // LANE

    @pl.when(pl.program_id(0) == 0)
    def _():
        state_ref[...] = jnp.zeros_like(state_ref)

    def input_map(k):
        slot = k % 2
        for b in range(bsz):
            swap_in_ref[slot, b * S5_PITCH:b * S5_PITCH + tl, :] = hs_ref[b, :, k * LANE:(k + 1) * LANE]
        u = jnp.concatenate([swap_in_ref[slot, pl.ds(t, bsz, stride=S5_PITCH), :] for t in range(tl)],
                            axis=0)
        u_ref[slot] = u
        bu_ref[slot] = _dot(u.astype(BF16), bmat_ref[k])

    input_map(0)
    for k in range(n_tiles):
        if k + 1 < n_tiles:
            input_map(k + 1)
        slot = k % 2
        buf = bu_ref.at[slot]
        a_re = jnp.broadcast_to(a_ref[k, 0:1, :], (bsz, S5_HALF))
        a_im = jnp.broadcast_to(a_ref[k, 1:2, :], (bsz, S5_HALF))
        st = state_ref[k]
        x_re, x_im = st[:, :S5_HALF], st[:, S5_HALF:]
        for t in range(tl):
            rs = slice(t * bsz, (t + 1) * bsz)
            bu = buf[rs, :]
            x_re, x_im = (a_re * x_re - a_im * x_im + bu[:, :S5_HALF],
                          a_re * x_im + a_im * x_re + bu[:, S5_HALF:])
            buf[rs, :] = jnp.concatenate([x_re, x_im], axis=1)
        state_ref[k] = jnp.concatenate([x_re, x_im], axis=1)

        lanes = slice(k * LANE, (k + 1) * LANE)
        y = jax.nn.gelu(_dot(buf[...].astype(BF16), cmat_ref[k]) + d_ref[:, lanes] * u_ref[slot])
        for t in range(tl):
            swap_out_ref[slot, pl.ds(t, bsz, stride=S5_PITCH), :] = y[t * bsz:(t + 1) * bsz, :]
        for b in range(bsz):
            y_ref[b, :, lanes] = swap_out_ref[slot, b * S5_PITCH:b * S5_PITCH + tl, :].astype(y_ref.dtype)


def _s5_scan(hs, bmat, cmat, a_bar, d_skip):
    bsz, lp, width = hs.shape
    tl = S5_TILE
    n_tiles = width // LANE
    return pl.pallas_call(
        _s5_kernel,
        grid=(lp // tl,),
        in_specs=[pl.BlockSpec((bsz, tl, width), lambda i: (0, i, 0)),
                  _const_spec(bmat.shape), _const_spec(cmat.shape),
                  _const_spec(a_bar.shape), _const_spec((1, width))],
        out_specs=pl.BlockSpec((bsz, tl, width), lambda i: (0, i, 0)),
        out_shape=jax.ShapeDtypeStruct((bsz, lp, width), BF16),
        scratch_shapes=[pltpu.VMEM((n_tiles, bsz, 2 * S5_HALF), F32),
                        pltpu.VMEM((2, tl * bsz, 2 * S5_HALF), F32),
                        pltpu.VMEM((2, tl * bsz, LANE), F32),
                        pltpu.VMEM((2, bsz * S5_PITCH, LANE), F32),
                        pltpu.VMEM((2, bsz * S5_PITCH, LANE), F32)],
        compiler_params=_params("arbitrary"),
        name="s5_scan",
    )(hs, bmat, cmat, a_bar, d_skip.reshape(1, width))


def _s5_discretise_kernel(lre_ref, lim_ref, step_ref, bre_ref, bim_ref,
                          are_ref, aim_ref, bbre_ref, bbim_ref):
    lam_re, lam_im = lre_ref[...], lim_ref[...]
    dt = jnp.exp(step_ref[...])
    mag = jnp.exp(lam_re * dt)
    a_re = mag * jnp.cos(lam_im * dt)
    a_im = mag * jnp.sin(lam_im * dt)
    den = lam_re * lam_re + lam_im * lam_im
    f_re = ((a_re - 1.0) * lam_re + a_im * lam_im) / den
    f_im = (a_im * lam_re - (a_re - 1.0) * lam_im) / den
    are_ref[...] = a_re
    aim_ref[...] = a_im
    bbre_ref[...] = f_re * bre_ref[...] - f_im * bim_ref[...]
    bbim_ref[...] = f_re * bim_ref[...] + f_im * bre_ref[...]


def _s5_discretise(lam_re, lam_im, log_step, b_re, b_im, c_re, c_im):
    n_groups = lam_re.shape[0]
    n_tiles = n_groups // GROUPS_PER_LANE_TILE
    flat = (n_groups, STATE * GROUP)
    spread = lambda t: jnp.repeat(t, GROUP, axis=1)
    a_re, a_im, bb_re, bb_im = pl.pallas_call(
        _s5_discretise_kernel,
        out_shape=[jax.ShapeDtypeStruct(flat, F32)] * 4,
        name="s5_discretise",
    )(spread(lam_re), spread(lam_im), log_step.reshape(n_groups, 1),
      b_re.reshape(flat), b_im.reshape(flat))
    a_re, a_im = a_re[:, ::GROUP], a_im[:, ::GROUP]
    bb_re = bb_re.reshape(n_groups, STATE, GROUP)
    bb_im = bb_im.reshape(n_groups, STATE, GROUP)
    eye = jnp.eye(GROUPS_PER_LANE_TILE, dtype=F32)

    def in_map(bb):
        t = bb.reshape(n_tiles, GROUPS_PER_LANE_TILE, STATE, GROUP)
        t = jnp.einsum("kgpc,gh->kgchp", t, eye)
        return t.reshape(n_tiles, LANE, S5_HALF)

    def out_map(cc):
        t = cc.reshape(n_tiles, GROUPS_PER_LANE_TILE, GROUP, STATE)
        t = jnp.einsum("kgcp,gh->kgphc", t, eye)
        return t.reshape(n_tiles, S5_HALF, LANE)

    bmat = jnp.concatenate([in_map(bb_re), in_map(bb_im)], axis=2).astype(BF16)
    cmat = jnp.concatenate([out_map(c_re), -out_map(c_im)], axis=1).astype(BF16)
    a_bar = jnp.stack([a_re.reshape(n_tiles, S5_HALF), a_im.reshape(n_tiles, S5_HALF)], axis=1)
    return bmat, cmat, a_bar


def kernel(x, meta_tokens, norm_mix_pre, norm_mix_post, norm_ffn_pre, norm_ffn_post, ffn_w_gate,
           ffn_w_up, ffn_w_down, hyb_w_in, hyb_conv_w, dn_a_log, dn_dt_bias, dn_out_norm, hyb_w_out,
           ssm_w_in, ssm_lambda_re, ssm_lambda_im, ssm_log_step, ssm_b_re, ssm_b_im, ssm_c_re,
           ssm_c_im, ssm_d, ssm_w_glu):
    bsz, seq, d = x.shape
    l_real = N_META + seq
    lp = -(-l_real // SEQ_ALIGN) * SEQ_ALIGN
    w_in = hyb_w_in[0]
    n_gate = 2 * HA
    w_cat = jnp.concatenate([
        w_in[:, :4 * DA],
        jnp.pad(w_in[:, 4 * DA:4 * DA + n_gate], ((0, 0), (0, LANE - n_gate))),
        w_in[:, 4 * DA + n_gate:]], axis=1).astype(BF16)
    gate_par = jnp.zeros((2, LANE), F32)
    gate_par = gate_par.at[0, HA:2 * HA].set(dn_a_log[0].astype(F32))
    gate_par = gate_par.at[1, HA:2 * HA].set(dn_dt_bias[0].astype(F32))
    h, qkv_a, zs_a, gates, qkv_b = _hyb_in(x, meta_tokens.astype(x.dtype), lp, norm_mix_pre[0], w_cat,
                                           hyb_conv_w[0].astype(F32), gate_par)
    o_a = _deltanet(qkv_a, zs_a, gates, dn_out_norm[0].reshape(1, HEAD_DIM).astype(F32))
    o_b = _stickbreak(qkv_b)
    h = _mixer_out_ffn(_hyb_out_ffn_kernel, [o_a, o_b], hyb_w_out[0].astype(BF16), h,
                       norm_mix_post[0], norm_ffn_pre[0], ffn_w_gate[0].astype(BF16),
                       ffn_w_up[0].astype(BF16), ffn_w_down[0].astype(BF16), norm_ffn_post[0])

    width = ssm_w_in.shape[2]
    (hs,) = _norm_proj(h, norm_mix_pre[1], ssm_w_in[0].astype(BF16), ((0, width),), (F32,))
    bmat, cmat, a_bar = _s5_discretise(ssm_lambda_re[0].astype(F32), ssm_lambda_im[0].astype(F32),
                                       ssm_log_step[0].astype(F32), ssm_b_re[0].astype(F32),
                                       ssm_b_im[0].astype(F32), ssm_c_re[0].astype(F32),
                                       ssm_c_im[0].astype(F32))
    y = _s5_scan(hs.reshape(bsz, lp, width), bmat, cmat, a_bar, ssm_d[0].astype(F32))
    return _mixer_out_ffn(_glu_out_ffn_kernel, [y], ssm_w_glu[0].astype(BF16),
                          h.reshape(bsz, lp, d), norm_mix_post[1], norm_ffn_pre[1],
                          ffn_w_gate[1].astype(BF16), ffn_w_up[1].astype(BF16),
                          ffn_w_down[1].astype(BF16), norm_ffn_post[1], keep=(N_META, seq))
```

```python
import functools

import jax
import jax.numpy as jnp
from jax import lax
from jax.experimental import pallas as pl
from jax.experimental.pallas import tpu as pltpu

N_META = 16
HEAD_DIM = 128
HA = 4
HB = 4
DA = HA * HEAD_DIM
DB = HB * HEAD_DIM
CONV_WIDTH = 4
GROUP = 16
STATE = 64
EPS = 1e-6

LANE = 128
SEQ_ALIGN = 128
DN_CHUNK = 128
DN_SEQS_PER_STEP = 4
INV_BASE = 16
ATT_BLOCK = 128
ATT_FIRST_BLOCKS = 3
FFN_SUBTILE = 256
S5_TILE = 64
S5_PITCH = S5_TILE + 8
GROUPS_PER_LANE_TILE = LANE // GROUP
S5_HALF = GROUPS_PER_LANE_TILE * STATE
VMEM_LIMIT = 56 * 1024 * 1024
EXP_UNDERFLOW = -105.0

F32 = jnp.float32
BF16 = jnp.bfloat16
HIGHEST = lax.Precision.HIGHEST


def _dot(a, b):
    return jnp.dot(a, b, preferred_element_type=F32)


def _dot_nt(a, b):
    return lax.dot_general(a, b, (((1,), (1,)), ((), ())), preferred_element_type=F32)


def _dot_tn(a, b):
    return lax.dot_general(a, b, (((0,), (0,)), ((), ())), preferred_element_type=F32)


def _dot_f32(a, b):
    return jnp.dot(a, b, preferred_element_type=F32, precision=HIGHEST)


def _rms(x, w):
    return x * lax.rsqrt(jnp.mean(x * x, axis=-1, keepdims=True) + EPS) * w


def _sigmoid(x):
    return 1.0 / (1.0 + jnp.exp(-x))


def _softplus(x):
    return jnp.maximum(x, 0.0) + jnp.log(1.0 + jnp.exp(-jnp.abs(x)))


def _const_spec(shape):
    zeros = (0,) * len(shape)
    return pl.BlockSpec(shape, lambda *_: zeros, pipeline_mode=pl.Buffered(1))


def _params(*sem):
    return pltpu.CompilerParams(dimension_semantics=sem, vmem_limit_bytes=VMEM_LIMIT)


HYB_COLS = (("qkv_a", 0, 3 * DA), ("z_a", 3 * DA, DA), ("gates", 4 * DA, LANE),
            ("qkv_b", 4 * DA + LANE, 3 * DB))
HALO = 8


def _sequence_tile(x_ref, meta_ref, tile, n_tiles, n_pad):
    xb = x_ref[...]
    tm = xb.shape[0]
    first = jnp.concatenate([meta_ref[...], xb[:tm - N_META]], axis=0)
    if n_pad:
        last = jnp.concatenate([xb[n_pad:], jnp.zeros((n_pad, xb.shape[1]), xb.dtype)], axis=0)
    else:
        last = xb
    return jnp.where(tile == 0, first, jnp.where(tile == n_tiles - 1, last, xb))


def _hyb_in_kernel(x_ref, meta_ref, nw_ref, w_ref, convw_ref, gpar_ref,
                   h_ref, qkv_ref, zs_ref, gate_ref, qkvb_ref, tail_ref, *, n_pad):
    cols = {name: slice(start, start + width) for name, start, width in HYB_COLS}

    @pl.when(pl.program_id(1) == 0)
    def _():
        tail_ref[...] = jnp.zeros_like(tail_ref)

    h0 = _sequence_tile(x_ref, meta_ref, pl.program_id(1), pl.num_programs(1), n_pad)
    h_ref[0] = h0
    y = _rms(h0, nw_ref[...]).astype(BF16)
    first_rows = lax.broadcasted_iota(jnp.int32, (HALO, DA), 0)

    def conv_silu(pre, cs):
        tail = tail_ref[:, cs]
        acc = pre * convw_ref[CONV_WIDTH - 1:CONV_WIDTH, cs]
        for s in range(1, CONV_WIDTH):
            rolled = pltpu.roll(pre, s, 0)
            head = jnp.where(first_rows < s, pltpu.roll(tail, s, 0), rolled[:HALO])
            shifted = jnp.concatenate([head, rolled[HALO:]], axis=0)
            acc = acc + shifted * convw_ref[CONV_WIDTH - 1 - s:CONV_WIDTH - s, cs]
        tail_ref[:, cs] = pre[pre.shape[0] - HALO:, :]
        return acc * _sigmoid(acc)

    def unit_heads(xc, base, gain):
        for h in range(HA):
            t = xc[:, h * HEAD_DIM:(h + 1) * HEAD_DIM]
            t = t * (lax.rsqrt(jnp.sum(t * t, axis=-1, keepdims=True) + EPS) * gain)
            qkv_ref[0, :, base + h * HEAD_DIM:base + (h + 1) * HEAD_DIM] = t.astype(qkv_ref.dtype)

    def post_q(pre):
        unit_heads(conv_silu(pre, slice(0, DA)), 0, HEAD_DIM ** -0.5)

    def post_k(pre):
        unit_heads(conv_silu(pre, slice(DA, 2 * DA)), DA, 1.0)

    def post_v(pre):
        qkv_ref[0, :, 2 * DA:] = conv_silu(pre, slice(2 * DA, 3 * DA)).astype(qkv_ref.dtype)

    def post_z(zlin):
        zs_ref[0] = (zlin * _sigmoid(zlin)).astype(zs_ref.dtype)

    def post_gates(ba):
        g = -jnp.exp(gpar_ref[0:1, :]) * _softplus(ba + gpar_ref[1:2, :])
        gate_ref[0] = jnp.where(lax.broadcasted_iota(jnp.int32, ba.shape, 1) < HA, _sigmoid(ba), g)

    def post_b(part):
        def store(pre):
            qkvb_ref[0, :, part * DB:(part + 1) * DB] = pre.astype(qkvb_ref.dtype)
        return store

    qa, qb = cols["qkv_a"].start, cols["qkv_b"].start
    attn = [(slice(qb + part * DB, qb + (part + 1) * DB), post_b(part)) for part in range(3)]
    stages = [(slice(qa, qa + DA), post_q), attn[0], (slice(qa + DA, qa + 2 * DA), post_k), attn[1],
              (slice(qa + 2 * DA, qa + 3 * DA), post_v), attn[2], (cols["z_a"], post_z),
              (cols["gates"], post_gates)]
    pending = _dot(y, w_ref[:, stages[0][0]])
    for idx, (_, post) in enumerate(stages):
        upcoming = _dot(y, w_ref[:, stages[idx + 1][0]]) if idx + 1 < len(stages) else None
        post(pending)
        pending = upcoming


def _hyb_in(x, meta, lp, norm_w, w, conv_w, gate_par):
    b, seq, d = x.shape
    n_pad = lp - (N_META + seq)
    tm = next(rows for rows in (384, 256, 128) if lp % rows == 0)
    assert lp // tm >= 2 and seq >= tm and n_pad % 8 == 0 and n_pad <= tm
    row = lambda width: pl.BlockSpec((1, tm, width), lambda bi, i: (bi, i, 0))
    n_tiles = lp // tm

    def x_rows(bi, i):
        start = (i * tm - N_META + N_META * (i == 0).astype(jnp.int32)
                 - n_pad * (i == n_tiles - 1).astype(jnp.int32))
        return bi, pl.multiple_of(start, HALO), 0

    x_spec = pl.BlockSpec((None, pl.Element(tm), pl.Element(d)), x_rows)
    widths = {name: width for name, _, width in HYB_COLS}
    return pl.pallas_call(
        functools.partial(_hyb_in_kernel, n_pad=n_pad),
        grid=(b, lp // tm),
        in_specs=[x_spec, _const_spec((N_META, d)), _const_spec((1, d)), _const_spec(w.shape),
                  _const_spec(conv_w.shape), _const_spec(gate_par.shape)],
        out_specs=[row(d), row(widths["qkv_a"]), row(widths["z_a"]), row(widths["gates"]),
                   row(widths["qkv_b"])],
        out_shape=[jax.ShapeDtypeStruct((b, lp, d), F32),
                   jax.ShapeDtypeStruct((b, lp, widths["qkv_a"]), BF16),
                   jax.ShapeDtypeStruct((b, lp, widths["z_a"]), BF16),
                   jax.ShapeDtypeStruct((b, lp, widths["gates"]), F32),
                   jax.ShapeDtypeStruct((b, lp, widths["qkv_b"]), BF16)],
        scratch_shapes=[pltpu.VMEM((HALO, widths["qkv_a"]), F32)],
        compiler_params=_params("parallel", "arbitrary"),
        name="hyb_in",
    )(x, meta, norm_w.reshape(1, d), w, conv_w, gate_par)


def _unit_lower_inverses(lowers, ri, ci):
    c = lowers[0].shape[0]

    def same_block(size):
        shift = size.bit_length() - 1
        return (ri >> shift) == (ci >> shift)

    def mm(a, b):
        return _dot(a.astype(BF16), b.astype(BF16))

    eye = jnp.where(ci == ri, 1.0, 0.0).astype(F32)
    in_base = same_block(INV_BASE)
    xs = [jnp.where(in_base, -low, 0.0) for low in lowers]
    ps = [eye + x for x in xs]
    power = 2
    while power < INV_BASE:
        xs = [mm(x, x) for x in xs]
        ps = [p + mm(p, x) for p, x in zip(ps, xs)]
        power *= 2
    size = INV_BASE
    while size < c:
        joins = jnp.logical_and(same_block(2 * size), jnp.logical_not(same_block(size)))
        ys = [mm(jnp.where(joins, low, 0.0), p) for low, p in zip(lowers, ps)]
        ps = [p - mm(p, y) for p, y in zip(ps, ys)]
        size *= 2
    return ps


def _deltanet_kernel(qkv_ref, zs_ref, gate_ref, onorm_ref, o_ref, state_ref):
    nseq, c = qkv_ref.shape[0], qkv_ref.shape[1]
    units = [(s, h) for s in range(nseq) for h in range(HA)]
    n = range(len(units))

    @pl.when(pl.program_id(1) == 0)
    def _():
        state_ref[...] = jnp.zeros_like(state_ref)

    ri = lax.broadcasted_iota(jnp.int32, (c, c), 0)
    ci = lax.broadcasted_iota(jnp.int32, (c, c), 1)
    incl = ci <= ri
    strict = ci < ri
    tril = jnp.where(incl, 1.0, 0.0).astype(F32)
    sel = jnp.where(lax.broadcasted_iota(jnp.int32, (8, LANE), 1)
                    == lax.broadcasted_iota(jnp.int32, (8, LANE), 0) + HA, 1.0, 0.0).astype(F32)
    gates = [gate_ref[s] for s in range(nseq)]
    g_cum = [_dot_f32(tril, g) for g in gates]
    g_rows = [lax.dot_general(sel, g, (((1,), (1,)), ((), ())), preferred_element_type=F32,
                              precision=HIGHEST) for g in g_cum]

    def head_cols(base, s, h):
        return qkv_ref[s, :, base + h * HEAD_DIM:base + (h + 1) * HEAD_DIM]

    q_bf = [head_cols(0, s, h) for s, h in units]
    k_bf = [head_cols(DA, s, h) for s, h in units]
    q = [t.astype(F32) for t in q_bf]
    k = [t.astype(F32) for t in k_bf]
    v = [head_cols(2 * DA, s, h).astype(F32) for s, h in units]
    beta = [gates[s][:, h:h + 1] for s, h in units]
    g_col = [g_cum[s][:, HA + h:HA + h + 1] for s, h in units]
    g_row = [g_rows[s][h:h + 1, :] for s, h in units]
    g_last = [g_cum[s][c - 1:c, HA + h:HA + h + 1] for s, h in units]
    decay = [jnp.where(incl, jnp.exp(jnp.where(incl, g_col[u] - g_row[u], 0.0)), 0.0) for u in n]
    exp_g = [jnp.exp(g) for g in g_col]
    kb = [k[u] * beta[u] for u in n]
    lower = [jnp.where(strict, _dot_nt(kb[u].astype(BF16), k_bf[u]) * decay[u], 0.0) for u in n]
    qk = [jnp.where(incl, _dot_nt(q_bf[u], k_bf[u]) * decay[u], 0.0).astype(BF16) for u in n]
    t_inv = _unit_lower_inverses(lower, ri, ci)
    rhs = [jnp.concatenate([v[u] * beta[u], kb[u] * exp_g[u]], axis=1).astype(BF16) for u in n]
    sol = [_dot(t_inv[u].astype(BF16), rhs[u]) for u in n]
    q_dec = [(q[u] * exp_g[u]).astype(BF16) for u in n]
    k_dec = [(k[u] * jnp.exp(g_last[u] - g_col[u])).astype(BF16) for u in n]

    state = [state_ref[u] for u in n]
    state_bf = [s.astype(BF16) for s in state]
    v_new = [(sol[u][:, :HEAD_DIM] - _dot(sol[u][:, HEAD_DIM:].astype(BF16), state_bf[u])).astype(BF16)
             for u in n]
    o = [_dot(q_dec[u], state_bf[u]) + _dot(qk[u], v_new[u]) for u in n]
    for u in n:
        state_ref[u] = state[u] * jnp.exp(g_last[u]) + _dot_tn(k_dec[u], v_new[u])

    onorm = onorm_ref[...]
    for u, (s, h) in enumerate(units):
        sl = slice(h * HEAD_DIM, (h + 1) * HEAD_DIM)
        o_ref[s, :, sl] = (_rms(o[u], onorm) * zs_ref[s, :, sl].astype(F32)).astype(o_ref.dtype)


def _deltanet(qkv, zs, gates, out_norm):
    b, lp, _ = qkv.shape
    c = DN_CHUNK
    nseq = DN_SEQS_PER_STEP
    return pl.pallas_call(
        _deltanet_kernel,
        grid=(b // nseq, lp // c),
        in_specs=[pl.BlockSpec((nseq, c, 3 * DA), lambda bi, i: (bi, i, 0)),
                  pl.BlockSpec((nseq, c, DA), lambda bi, i: (bi, i, 0)),
                  pl.BlockSpec((nseq, c, LANE), lambda bi, i: (bi, i, 0)),
                  _const_spec((1, HEAD_DIM))],
        out_specs=pl.BlockSpec((nseq, c, DA), lambda bi, i: (bi, i, 0)),
        out_shape=jax.ShapeDtypeStruct((b, lp, DA), BF16),
        scratch_shapes=[pltpu.VMEM((nseq * HA, HEAD_DIM, HEAD_DIM), F32)],
        compiler_params=_params("parallel", "arbitrary"),
        name="deltanet",
    )(qkv, zs, gates, out_norm)


def _stickbreak_visit(q_ref, k_ref, v_ref, heads, suffix, first, count, laters, diag_mask):
    tk = suffix.shape[0]
    n_heads = range(len(heads))
    chains = [(b, h) for b in range(count) for h in n_heads]
    index = [first - b for b in range(count)]
    starts = [pl.multiple_of(jnp.maximum(j, 0) * tk, tk) for j in index]

    def masked(c, t):
        b, _ = chains[c]
        if diag_mask is None:
            return t
        return jnp.where(diag_mask if b == 0 else index[b] >= 0, t, 0.0)

    n = range(len(chains))
    qs = [q_ref[0, :, sl] for sl in heads]
    zq = [_dot_nt(qs[h], k_ref[0, pl.ds(starts[b], tk), heads[h]]) * (HEAD_DIM ** -0.5)
          for b, h in chains]
    log_beta = [jnp.minimum(z, 0.0) - jnp.log(1.0 + jnp.exp(-jnp.abs(z))) for z in zq]
    log_1m = [masked(c, log_beta[c] - zq[c]) for c in n]
    hi = [t.astype(BF16) for t in log_1m]
    lo = [(log_1m[c] - hi[c].astype(F32)).astype(BF16) for c in n]
    suf = [_dot(hi[c], suffix) + _dot(lo[c], suffix) for c in n]
    entering, running = {}, list(laters)
    for c, (b, h) in enumerate(chains):
        entering[c] = running[h]
        running[h] = running[h] + suf[c][:, 0:1]
    wgt = [masked(c, jnp.exp(log_beta[c] + (suf[c] - log_1m[c]) + entering[c])).astype(BF16) for c in n]
    pv = [_dot(wgt[c], v_ref[0, pl.ds(starts[b], tk), heads[h]]) for c, (b, h) in enumerate(chains)]
    contrib = tuple(functools.reduce(lambda x, y: x + y, [pv[c] for c in n if chains[c][1] == h])
                    for h in n_heads)
    largest = jnp.max(functools.reduce(jnp.maximum, running))
    return contrib, tuple(running), largest


def _stickbreak_kernel(q_ref, k_ref, v_ref, o_ref):
    tq = q_ref.shape[1]
    tk = tq
    i = pl.program_id(1)
    heads = [slice(h * HEAD_DIM, (h + 1) * HEAD_DIM) for h in range(HB)]
    suffix = jnp.where(lax.broadcasted_iota(jnp.int32, (tk, tk), 0)
                       >= lax.broadcasted_iota(jnp.int32, (tk, tk), 1), 1.0, 0.0).astype(BF16)
    strictly_causal = (lax.broadcasted_iota(jnp.int32, (tq, tk), 1)
                       < lax.broadcasted_iota(jnp.int32, (tq, tk), 0))
    accs, laters, largest = _stickbreak_visit(
        q_ref, k_ref, v_ref, heads, suffix, i, ATT_FIRST_BLOCKS,
        (jnp.zeros((tq, 1), F32),) * HB, strictly_causal)

    def more(carry):
        j, largest, _, _ = carry
        return jnp.logical_and(j >= 0, largest > EXP_UNDERFLOW)

    def body(carry):
        j, _, accs, laters = carry
        contrib, laters, largest = _stickbreak_visit(q_ref, k_ref, v_ref, heads, suffix, j, 1,
                                                     laters, None)
        return j - 1, largest, tuple(a + c for a, c in zip(accs, contrib)), laters

    _, _, accs, _ = lax.while_loop(more, body, (i - ATT_FIRST_BLOCKS, largest, accs, laters))
    for sl, acc in zip(heads, accs):
        o_ref[0, :, sl] = acc.astype(o_ref.dtype)


def _stickbreak(qkv):
    b, lp, _ = qkv.shape
    tq = ATT_BLOCK
    return pl.pallas_call(
        _stickbreak_kernel,
        grid=(b, lp // tq),
        in_specs=[pl.BlockSpec((1, tq, DB), lambda bi, i: (bi, i, 0)),
                  pl.BlockSpec((1, lp, DB), lambda bi, i: (bi, 0, 1)),
                  pl.BlockSpec((1, lp, DB), lambda bi, i: (bi, 0, 2))],
        out_specs=pl.BlockSpec((1, tq, DB), lambda bi, i: (bi, i, 0)),
        out_shape=jax.ShapeDtypeStruct((b, lp, DB), BF16),
        compiler_params=_params("parallel", "arbitrary"),
        name="stickbreak",
    )(qkv, qkv, qkv)


def _row_subtiles(rows):
    return [slice(r, r + FFN_SUBTILE) for r in range(0, rows, FFN_SUBTILE)]


def _residual_ffn(ms, subs, h_ref, npost_ref, nfpre_ref, wg_ref, wu_ref, wd_ref, nfpost_ref, out_ref):
    h1 = [h_ref[r, :] + _rms(m, npost_ref[...]) for m, r in zip(ms, subs)]
    hn = [_rms(t, nfpre_ref[...]).astype(BF16) for t in h1]
    g = [_dot(t, wg_ref[...]) for t in hn]
    u = [_dot(t, wu_ref[...]) for t in hn]
    a = [(gi * _sigmoid(gi) * ui).astype(BF16) for gi, ui in zip(g, u)]
    f = [_dot(t, wd_ref[...]) for t in a]
    for r, h1i, fi in zip(subs, h1, f):
        out_ref[r, :] = h1i + _rms(fi, nfpost_ref[...])


def _hyb_out_ffn_kernel(oa_ref, ob_ref, wo_ref, *rest):
    subs = _row_subtiles(oa_ref.shape[0])
    ms = [_dot(oa_ref[r, :], wo_ref[:DA, :]) + _dot(ob_ref[r, :], wo_ref[DA:, :]) for r in subs]
    _residual_ffn(ms, subs, *rest)


def _glu_out_ffn_kernel(y_ref, wglu_ref, *rest):
    d = wglu_ref.shape[1] // 2
    subs = _row_subtiles(y_ref.shape[0])
    yv = [_dot(y_ref[r, :], wglu_ref[...]) for r in subs]
    ms = [t[:, :d] * _sigmoid(t[:, d:]) for t in yv]
    _residual_ffn(ms, subs, *rest)


def _mixer_out_ffn(kernel_fn, acts, w_mix, h, npost, nfpre, wg, wu, wd, nfpost, keep=None):
    b, lp, d = h.shape
    dff = wg.shape[1]
    tm = 2 * FFN_SUBTILE
    weights = [_const_spec(w_mix.shape)]
    tail = [_const_spec((1, d)), _const_spec((1, d)), _const_spec((d, dff)), _const_spec((d, dff)),
            _const_spec((dff, d)), _const_spec((1, d))]
    if keep is None:
        rows = b * lp
        acts = [a.reshape(rows, a.shape[2]) for a in acts]
        h = h.reshape(rows, d)
        grid = (rows // tm,)
        row = lambda width: pl.BlockSpec((tm, width), lambda i: (i, 0))
        out_spec, out_shape, sem = row(d), (rows, d), ("parallel",)
    else:
        first, n_rows = keep
        assert first % N_META == 0 and tm % N_META == 0
        grid = (b, n_rows // tm)
        row = lambda width: pl.BlockSpec((None, pl.Element(tm), pl.Element(width)),
                                         lambda bi, i: (bi, pl.multiple_of(first + i * tm, N_META), 0))
        out_spec = pl.BlockSpec((None, tm, d), lambda bi, i: (bi, i, 0))
        out_shape, sem = (b, n_rows, d), ("parallel", "parallel")
    assert grid[-1] * tm == (b * lp if keep is None else keep[1])
    return pl.pallas_call(
        kernel_fn,
        grid=grid,
        in_specs=[row(a.shape[-1]) for a in acts] + weights + [row(d)] + tail,
        out_specs=out_spec,
        out_shape=jax.ShapeDtypeStruct(out_shape, F32),
        compiler_params=_params(*sem),
        name="mixer_out_ffn",
    )(*acts, w_mix, h, npost.reshape(1, d), nfpre.reshape(1, d), wg, wu, wd, nfpost.reshape(1, d))


def _s5_kernel(h_ref, nw_ref, win_ref, bmat_ref, cmat_ref, a_ref, d_ref, y_ref, state_ref, bu_ref,
               u_ref, swap_in_ref, swap_out_ref, hn_ref, hs_ref):
    bsz, tl, dm = h_ref.shape
    width = win_ref.shape[1]
    n_tiles = width // LANE
    pair = 2 * LANE

    @pl.when(pl.program_id(0) == 0)
    def _():
        state_ref[...] = jnp.zeros_like(state_ref)

    hn_ref[...] = _rms(h_ref[...].reshape(bsz * tl, dm), nw_ref[...]).astype(BF16)

    def input_map(k):
        slot = k % 2
        if k % 2 == 0:
            cols = slice(k * LANE, k * LANE + pair)
            hs_ref[:, :, cols] = _dot(hn_ref[...], win_ref[:, cols]).reshape(bsz, tl, pair)
        for b in range(bsz):
            swap_in_ref[slot, b * S5_PITCH:b * S5_PITCH + tl, :] = hs_ref[b, :, k * LANE:(k + 1) * LANE]
        u = jnp.concatenate([swap_in_ref[slot, pl.ds(t, bsz, stride=S5_PITCH), :] for t in range(tl)],
                            axis=0)
        u_ref[slot] = u
        bu_ref[slot] = _dot(u.astype(BF16), bmat_ref[k])

    input_map(0)
    for k in range(n_tiles):
        if k + 1 < n_tiles:
            input_map(k + 1)
        slot = k % 2
        buf = bu_ref.at[slot]
        a_re = jnp.broadcast_to(a_ref[k, 0:1, :], (bsz, S5_HALF))
        a_im = jnp.broadcast_to(a_ref[k, 1:2, :], (bsz, S5_HALF))
        st = state_ref[k]
        x_re, x_im = st[:, :S5_HALF], st[:, S5_HALF:]
        for t in range(tl):
            rs = slice(t * bsz, (t + 1) * bsz)
            bu = buf[rs, :]
            x_re, x_im = (a_re * x_re - a_im * x_im + bu[:, :S5_HALF],
                          a_re * x_im + a_im * x_re + bu[:, S5_HALF:])
            buf[rs, :] = jnp.concatenate([x_re, x_im], axis=1)
        state_ref[k] = jnp.concatenate([x_re, x_im], axis=1)

        lanes = slice(k * LANE, (k + 1) * LANE)
        y = jax.nn.gelu(_dot(buf[...].astype(BF16), cmat_ref[k]) + d_ref[:, lanes] * u_ref[slot])
        for t in range(tl):
            swap_out_ref[slot, pl.ds(t, bsz, stride=S5_PITCH), :] = y[t * bsz:(t + 1) * bsz, :]
        for b in range(bsz):
            y_ref[b, :, lanes] = swap_out_ref[slot, b * S5_PITCH:b * S5_PITCH + tl, :].astype(y_ref.dtype)


def _s5_mixer(h, norm_w, w_in, bmat, cmat, a_bar, d_skip):
    bsz, lp, dm = h.shape
    width = w_in.shape[1]
    tl = S5_TILE
    n_tiles = width // LANE
    assert n_tiles % 2 == 0
    return pl.pallas_call(
        _s5_kernel,
        grid=(lp // tl,),
        in_specs=[pl.BlockSpec((bsz, tl, dm), lambda i: (0, i, 0)),
                  _const_spec((1, dm)), _const_spec(w_in.shape),
                  _const_spec(bmat.shape), _const_spec(cmat.shape),
                  _const_spec(a_bar.shape), _const_spec((1, width))],
        out_specs=pl.BlockSpec((bsz, tl, width), lambda i: (0, i, 0)),
        out_shape=jax.ShapeDtypeStruct((bsz, lp, width), BF16),
        scratch_shapes=[pltpu.VMEM((n_tiles, bsz, 2 * S5_HALF), F32),
                        pltpu.VMEM((2, tl * bsz, 2 * S5_HALF), F32),
                        pltpu.VMEM((2, tl * bsz, LANE), F32),
                        pltpu.VMEM((2, bsz * S5_PITCH, LANE), F32),
                        pltpu.VMEM((2, bsz * S5_PITCH, LANE), F32),
                        pltpu.VMEM((tl * bsz, dm), BF16),
                        pltpu.VMEM((bsz, tl, width), F32)],
        compiler_params=_params("arbitrary"),
        name="s5_mixer",
    )(h, norm_w.reshape(1, dm), w_in, bmat, cmat, a_bar, d_skip.reshape(1, width))


def _s5_discretise_kernel(lre_ref, lim_ref, step_ref, bre_ref, bim_ref,
                          are_ref, aim_ref, bbre_ref, bbim_ref):
    lam_re, lam_im = lre_ref[...], lim_ref[...]
    dt = jnp.exp(step_ref[...])
    mag = jnp.exp(lam_re * dt)
    a_re = mag * jnp.cos(lam_im * dt)
    a_im = mag * jnp.sin(lam_im * dt)
    den = lam_re * lam_re + lam_im * lam_im
    f_re = ((a_re - 1.0) * lam_re + a_im * lam_im) / den
    f_im = (a_im * lam_re - (a_re - 1.0) * lam_im) / den
    are_ref[...] = a_re
    aim_ref[...] = a_im
    bbre_ref[...] = f_re * bre_ref[...] - f_im * bim_ref[...]
    bbim_ref[...] = f_re * bim_ref[...] + f_im * bre_ref[...]


def _s5_discretise(lam_re, lam_im, log_step, b_re, b_im, c_re, c_im):
    n_groups = lam_re.shape[0]
    n_tiles = n_groups // GROUPS_PER_LANE_TILE
    flat = (n_groups, STATE * GROUP)
    spread = lambda t: jnp.repeat(t, GROUP, axis=1)
    a_re, a_im, bb_re, bb_im = pl.pallas_call(
        _s5_discretise_kernel,
        out_shape=[jax.ShapeDtypeStruct(flat, F32)] * 4,
        name="s5_discretise",
    )(spread(lam_re), spread(lam_im), log_step.reshape(n_groups, 1),
      b_re.reshape(flat), b_im.reshape(flat))
    a_re, a_im = a_re[:, ::GROUP], a_im[:, ::GROUP]
    bb_re = bb_re.reshape(n_groups, STATE, GROUP)
    bb_im = bb_im.reshape(n_groups, STATE, GROUP)
    eye = jnp.eye(GROUPS_PER_LANE_TILE, dtype=F32)

    def in_map(bb):
        t = bb.reshape(n_tiles, GROUPS_PER_LANE_TILE, STATE, GROUP)
        t = jnp.einsum("kgpc,gh->kgchp", t, eye)
        return t.reshape(n_tiles, LANE, S5_HALF)

    def out_map(cc):
        t = cc.reshape(n_tiles, GROUPS_PER_LANE_TILE, GROUP, STATE)
        t = jnp.einsum("kgcp,gh->kgphc", t, eye)
        return t.reshape(n_tiles, S5_HALF, LANE)

    bmat = jnp.concatenate([in_map(bb_re), in_map(bb_im)], axis=2).astype(BF16)
    cmat = jnp.concatenate([out_map(c_re), -out_map(c_im)], axis=1).astype(BF16)
    a_bar = jnp.stack([a_re.reshape(n_tiles, S5_HALF), a_im.reshape(n_tiles, S5_HALF)], axis=1)
    return bmat, cmat, a_bar


def kernel(x, meta_tokens, norm_mix_pre, norm_mix_post, norm_ffn_pre, norm_ffn_post, ffn_w_gate,
           ffn_w_up, ffn_w_down, hyb_w_in, hyb_conv_w, dn_a_log, dn_dt_bias, dn_out_norm, hyb_w_out,
           ssm_w_in, ssm_lambda_re, ssm_lambda_im, ssm_log_step, ssm_b_re, ssm_b_im, ssm_c_re,
           ssm_c_im, ssm_d, ssm_w_glu):
    bsz, seq, d = x.shape
    l_real = N_META + seq
    lp = -(-l_real // SEQ_ALIGN) * SEQ_ALIGN
    w_in = hyb_w_in[0]
    n_gate = 2 * HA
    w_cat = jnp.concatenate([
        w_in[:, :4 * DA],
        jnp.pad(w_in[:, 4 * DA:4 * DA + n_gate], ((0, 0), (0, LANE - n_gate))),
        w_in[:, 4 * DA + n_gate:]], axis=1).astype(BF16)
    gate_par = jnp.zeros((2, LANE), F32)
    gate_par = gate_par.at[0, HA:2 * HA].set(dn_a_log[0].astype(F32))
    gate_par = gate_par.at[1, HA:2 * HA].set(dn_dt_bias[0].astype(F32))
    h, qkv_a, zs_a, gates, qkv_b = _hyb_in(x, meta_tokens.astype(x.dtype), lp, norm_mix_pre[0], w_cat,
                                           hyb_conv_w[0].astype(F32), gate_par)
    o_a = _deltanet(qkv_a, zs_a, gates, dn_out_norm[0].reshape(1, HEAD_DIM).astype(F32))
    o_b = _stickbreak(qkv_b)
    h = _mixer_out_ffn(_hyb_out_ffn_kernel, [o_a, o_b], hyb_w_out[0].astype(BF16), h,
                       norm_mix_post[0], norm_ffn_pre[0], ffn_w_gate[0].astype(BF16),
                       ffn_w_up[0].astype(BF16), ffn_w_down[0].astype(BF16), norm_ffn_post[0])

    h = h.reshape(bsz, lp, d)
    bmat, cmat, a_bar = _s5_discretise(ssm_lambda_re[0].astype(F32), ssm_lambda_im[0].astype(F32),
                                       ssm_log_step[0].astype(F32), ssm_b_re[0].astype(F32),
                                       ssm_b_im[0].astype(F32), ssm_c_re[0].astype(F32),
                                       ssm_c_im[0].astype(F32))
    y = _s5_mixer(h, norm_mix_pre[1], ssm_w_in[0].astype(BF16), bmat, cmat, a_bar,
                  ssm_d[0].astype(F32))
    return _mixer_out_ffn(_glu_out_ffn_kernel, [y], ssm_w_glu[0].astype(BF16),
                          h, norm_mix_post[1], norm_ffn_pre[1],
                          ffn_w_gate[1].astype(BF16), ffn_w_up[1].astype(BF16),
                          ffn_w_down[1].astype(BF16), norm_ffn_post[1], keep=(N_META, seq))
```

```python
import functools

import jax
import jax.numpy as jnp
from jax import lax
from jax.experimental import pallas as pl
from jax.experimental.pallas import tpu as pltpu

N_META = 16
HEAD_DIM = 128
HA = 4
HB = 4
DA = HA * HEAD_DIM
DB = HB * HEAD_DIM
CONV_WIDTH = 4
GROUP = 16
STATE = 64
EPS = 1e-6

LANE = 128
SEQ_ALIGN = 128
DN_CHUNK = 128
DN_SEQS_PER_STEP = 4
INV_BASE = 16
ATT_BLOCK = 128
ATT_SEQS_PER_STEP = 2
ATT_FIRST_BLOCKS = 3
FFN_SUBTILE = 256
S5_TILE = 64
S5_PITCH = S5_TILE + 8
GROUPS_PER_LANE_TILE = LANE // GROUP
S5_HALF = GROUPS_PER_LANE_TILE * STATE
VMEM_LIMIT = 56 * 1024 * 1024
EXP_UNDERFLOW = -105.0

F32 = jnp.float32
BF16 = jnp.bfloat16
HIGHEST = lax.Precision.HIGHEST


def _dot(a, b):
    return jnp.dot(a, b, preferred_element_type=F32)


def _dot_nt(a, b):
    return lax.dot_general(a, b, (((1,), (1,)), ((), ())), preferred_element_type=F32)


def _dot_tn(a, b):
    return lax.dot_general(a, b, (((0,), (0,)), ((), ())), preferred_element_type=F32)


def _dot_f32(a, b):
    return jnp.dot(a, b, preferred_element_type=F32, precision=HIGHEST)


def _rms(x, w):
    return x * lax.rsqrt(jnp.mean(x * x, axis=-1, keepdims=True) + EPS) * w


def _sigmoid(x):
    return 1.0 / (1.0 + jnp.exp(-x))


def _softplus(x):
    return jnp.maximum(x, 0.0) + jnp.log(1.0 + jnp.exp(-jnp.abs(x)))


def _const_spec(shape):
    zeros = (0,) * len(shape)
    return pl.BlockSpec(shape, lambda *_: zeros, pipeline_mode=pl.Buffered(1))


def _params(*sem):
    return pltpu.CompilerParams(dimension_semantics=sem, vmem_limit_bytes=VMEM_LIMIT)


HYB_COLS = (("qkv_a", 0, 3 * DA), ("z_a", 3 * DA, DA), ("gates", 4 * DA, LANE),
            ("qkv_b", 4 * DA + LANE, 3 * DB))
HALO = 8


def _sequence_tile(x_ref, meta_ref, tile, n_tiles, n_pad):
    xb = x_ref[...]
    tm = xb.shape[0]
    first = jnp.concatenate([meta_ref[...], xb[:tm - N_META]], axis=0)
    if n_pad:
        last = jnp.concatenate([xb[n_pad:], jnp.zeros((n_pad, xb.shape[1]), xb.dtype)], axis=0)
    else:
        last = xb
    return jnp.where(tile == 0, first, jnp.where(tile == n_tiles - 1, last, xb))


def _hyb_in_kernel(x_ref, meta_ref, nw_ref, w_ref, convw_ref, gpar_ref,
                   h_ref, qkv_ref, zs_ref, gate_ref, qkvb_ref, tail_ref, *, n_pad):
    cols = {name: slice(start, start + width) for name, start, width in HYB_COLS}

    @pl.when(pl.program_id(1) == 0)
    def _():
        tail_ref[...] = jnp.zeros_like(tail_ref)

    h0 = _sequence_tile(x_ref, meta_ref, pl.program_id(1), pl.num_programs(1), n_pad)
    h_ref[0] = h0
    y = _rms(h0, nw_ref[...]).astype(BF16)
    first_rows = lax.broadcasted_iota(jnp.int32, (HALO, DA), 0)

    def conv_silu(pre, cs):
        tail = tail_ref[:, cs]
        acc = pre * convw_ref[CONV_WIDTH - 1:CONV_WIDTH, cs]
        for s in range(1, CONV_WIDTH):
            rolled = pltpu.roll(pre, s, 0)
            head = jnp.where(first_rows < s, pltpu.roll(tail, s, 0), rolled[:HALO])
            shifted = jnp.concatenate([head, rolled[HALO:]], axis=0)
            acc = acc + shifted * convw_ref[CONV_WIDTH - 1 - s:CONV_WIDTH - s, cs]
        tail_ref[:, cs] = pre[pre.shape[0] - HALO:, :]
        return acc * _sigmoid(acc)

    def unit_heads(xc, base, gain):
        for h in range(HA):
            t = xc[:, h * HEAD_DIM:(h + 1) * HEAD_DIM]
            t = t * (lax.rsqrt(jnp.sum(t * t, axis=-1, keepdims=True) + EPS) * gain)
            qkv_ref[0, :, base + h * HEAD_DIM:base + (h + 1) * HEAD_DIM] = t.astype(qkv_ref.dtype)

    def post_q(pre):
        unit_heads(conv_silu(pre, slice(0, DA)), 0, HEAD_DIM ** -0.5)

    def post_k(pre):
        unit_heads(conv_silu(pre, slice(DA, 2 * DA)), DA, 1.0)

    def post_v(pre):
        qkv_ref[0, :, 2 * DA:] = conv_silu(pre, slice(2 * DA, 3 * DA)).astype(qkv_ref.dtype)

    def post_z(zlin):
        zs_ref[0] = (zlin * _sigmoid(zlin)).astype(zs_ref.dtype)

    def post_gates(ba):
        g = -jnp.exp(gpar_ref[0:1, :]) * _softplus(ba + gpar_ref[1:2, :])
        gate_ref[0] = jnp.where(lax.broadcasted_iota(jnp.int32, ba.shape, 1) < HA, _sigmoid(ba), g)

    def post_b(part):
        def store(pre):
            qkvb_ref[0, :, part * DB:(part + 1) * DB] = pre.astype(qkvb_ref.dtype)
        return store

    qa, qb = cols["qkv_a"].start, cols["qkv_b"].start
    attn = [(slice(qb + part * DB, qb + (part + 1) * DB), post_b(part)) for part in range(3)]
    stages = [(slice(qa, qa + DA), post_q), attn[0], (slice(qa + DA, qa + 2 * DA), post_k), attn[1],
              (slice(qa + 2 * DA, qa + 3 * DA), post_v), attn[2], (cols["z_a"], post_z),
              (cols["gates"], post_gates)]
    pending = _dot(y, w_ref[:, stages[0][0]])
    for idx, (_, post) in enumerate(stages):
        upcoming = _dot(y, w_ref[:, stages[idx + 1][0]]) if idx + 1 < len(stages) else None
        post(pending)
        pending = upcoming


def _hyb_in(x, meta, lp, norm_w, w, conv_w, gate_par):
    b, seq, d = x.shape
    n_pad = lp - (N_META + seq)
    tm = next(rows for rows in (384, 256, 128) if lp % rows == 0)
    assert lp // tm >= 2 and seq >= tm and n_pad % 8 == 0 and n_pad <= tm
    row = lambda width: pl.BlockSpec((1, tm, width), lambda bi, i: (bi, i, 0))
    n_tiles = lp // tm

    def x_rows(bi, i):
        start = (i * tm - N_META + N_META * (i == 0).astype(jnp.int32)
                 - n_pad * (i == n_tiles - 1).astype(jnp.int32))
        return bi, pl.multiple_of(start, HALO), 0

    x_spec = pl.BlockSpec((None, pl.Element(tm), pl.Element(d)), x_rows)
    widths = {name: width for name, _, width in HYB_COLS}
    return pl.pallas_call(
        functools.partial(_hyb_in_kernel, n_pad=n_pad),
        grid=(b, lp // tm),
        in_specs=[x_spec, _const_spec((N_META, d)), _const_spec((1, d)), _const_spec(w.shape),
                  _const_spec(conv_w.shape), _const_spec(gate_par.shape)],
        out_specs=[row(d), row(widths["qkv_a"]), row(widths["z_a"]), row(widths["gates"]),
                   row(widths["qkv_b"])],
        out_shape=[jax.ShapeDtypeStruct((b, lp, d), F32),
                   jax.ShapeDtypeStruct((b, lp, widths["qkv_a"]), BF16),
                   jax.ShapeDtypeStruct((b, lp, widths["z_a"]), BF16),
                   jax.ShapeDtypeStruct((b, lp, widths["gates"]), F32),
                   jax.ShapeDtypeStruct((b, lp, widths["qkv_b"]), BF16)],
        scratch_shapes=[pltpu.VMEM((HALO, widths["qkv_a"]), F32)],
        compiler_params=_params("parallel", "arbitrary"),
        name="hyb_in",
    )(x, meta, norm_w.reshape(1, d), w, conv_w, gate_par)


def _run_stages(stage_generator):
    try:
        while True:
            next(stage_generator)
    except StopIteration as stop:
        return stop.value


def _unit_lower_inverses(lowers, ri, ci):
    c = lowers[0].shape[0]

    def same_block(size):
        shift = size.bit_length() - 1
        return (ri >> shift) == (ci >> shift)

    def mm(a, b):
        return _dot(a.astype(BF16), b.astype(BF16))

    eye = jnp.where(ci == ri, 1.0, 0.0).astype(F32)
    in_base = same_block(INV_BASE)
    xs = [jnp.where(in_base, -low, 0.0) for low in lowers]
    ps = [eye + x for x in xs]
    power = 2
    while power < INV_BASE:
        xs = [mm(x, x) for x in xs]
        yield
        ps = [p + mm(p, x) for p, x in zip(ps, xs)]
        yield
        power *= 2
    size = INV_BASE
    while size < c:
        joins = jnp.logical_and(same_block(2 * size), jnp.logical_not(same_block(size)))
        ys = [mm(jnp.where(joins, low, 0.0), p) for low, p in zip(lowers, ps)]
        yield
        ps = [p - mm(p, y) for p, y in zip(ps, ys)]
        yield
        size *= 2
    return ps


def _deltanet_stages(qkv_ref, zs_ref, gate_ref, onorm_ref, o_ref, state_ref):
    nseq, c = qkv_ref.shape[0], qkv_ref.shape[1]
    units = [(s, h) for s in range(nseq) for h in range(HA)]
    n = range(len(units))

    @pl.when(pl.program_id(1) == 0)
    def _():
        state_ref[...] = jnp.zeros_like(state_ref)

    ri = lax.broadcasted_iota(jnp.int32, (c, c), 0)
    ci = lax.broadcasted_iota(jnp.int32, (c, c), 1)
    incl = ci <= ri
    strict = ci < ri
    tril = jnp.where(incl, 1.0, 0.0).astype(F32)
    sel = jnp.where(lax.broadcasted_iota(jnp.int32, (8, LANE), 1)
                    == lax.broadcasted_iota(jnp.int32, (8, LANE), 0) + HA, 1.0, 0.0).astype(F32)
    gates = [gate_ref[s] for s in range(nseq)]
    g_cum = [_dot_f32(tril, g) for g in gates]
    yield
    g_rows = [lax.dot_general(sel, g, (((1,), (1,)), ((), ())), preferred_element_type=F32,
                              precision=HIGHEST) for g in g_cum]
    yield

    def head_cols(base, s, h):
        return qkv_ref[s, :, base + h * HEAD_DIM:base + (h + 1) * HEAD_DIM]

    q_bf = [head_cols(0, s, h) for s, h in units]
    k_bf = [head_cols(DA, s, h) for s, h in units]
    q = [t.astype(F32) for t in q_bf]
    k = [t.astype(F32) for t in k_bf]
    v = [head_cols(2 * DA, s, h).astype(F32) for s, h in units]
    beta = [gates[s][:, h:h + 1] for s, h in units]
    g_col = [g_cum[s][:, HA + h:HA + h + 1] for s, h in units]
    g_row = [g_rows[s][h:h + 1, :] for s, h in units]
    g_last = [g_cum[s][c - 1:c, HA + h:HA + h + 1] for s, h in units]
    decay = [jnp.where(incl, jnp.exp(jnp.where(incl, g_col[u] - g_row[u], 0.0)), 0.0) for u in n]
    exp_g = [jnp.exp(g) for g in g_col]
    kb = [k[u] * beta[u] for u in n]
    yield
    lower = [jnp.where(strict, _dot_nt(kb[u].astype(BF16), k_bf[u]) * decay[u], 0.0) for u in n]
    yield
    qk = [jnp.where(incl, _dot_nt(q_bf[u], k_bf[u]) * decay[u], 0.0).astype(BF16) for u in n]
    yield
    t_inv = yield from _unit_lower_inverses(lower, ri, ci)
    rhs = [jnp.concatenate([v[u] * beta[u], kb[u] * exp_g[u]], axis=1).astype(BF16) for u in n]
    sol = [_dot(t_inv[u].astype(BF16), rhs[u]) for u in n]
    yield
    q_dec = [(q[u] * exp_g[u]).astype(BF16) for u in n]
    k_dec = [(k[u] * jnp.exp(g_last[u] - g_col[u])).astype(BF16) for u in n]

    state = [state_ref[u] for u in n]
    state_bf = [s.astype(BF16) for s in state]
    v_new = [(sol[u][:, :HEAD_DIM] - _dot(sol[u][:, HEAD_DIM:].astype(BF16), state_bf[u])).astype(BF16)
             for u in n]
    yield
    o = [_dot(q_dec[u], state_bf[u]) + _dot(qk[u], v_new[u]) for u in n]
    yield
    for u in n:
        state_ref[u] = state[u] * jnp.exp(g_last[u]) + _dot_tn(k_dec[u], v_new[u])
    yield

    onorm = onorm_ref[...]
    for u, (s, h) in enumerate(units):
        sl = slice(h * HEAD_DIM, (h + 1) * HEAD_DIM)
        o_ref[s, :, sl] = (_rms(o[u], onorm) * zs_ref[s, :, sl].astype(F32)).astype(o_ref.dtype)


def _deltanet_kernel(*refs):
    _run_stages(_deltanet_stages(*refs))


def _deltanet(qkv, zs, gates, out_norm):
    b, lp, _ = qkv.shape
    c = DN_CHUNK
    nseq = DN_SEQS_PER_STEP
    return pl.pallas_call(
        _deltanet_kernel,
        grid=(b // nseq, lp // c),
        in_specs=[pl.BlockSpec((nseq, c, 3 * DA), lambda bi, i: (bi, i, 0)),
                  pl.BlockSpec((nseq, c, DA), lambda bi, i: (bi, i, 0)),
                  pl.BlockSpec((nseq, c, LANE), lambda bi, i: (bi, i, 0)),
                  _const_spec((1, HEAD_DIM))],
        out_specs=pl.BlockSpec((nseq, c, DA), lambda bi, i: (bi, i, 0)),
        out_shape=jax.ShapeDtypeStruct((b, lp, DA), BF16),
        scratch_shapes=[pltpu.VMEM((nseq * HA, HEAD_DIM, HEAD_DIM), F32)],
        compiler_params=_params("parallel", "arbitrary"),
        name="deltanet",
    )(qkv, zs, gates, out_norm)


def _stickbreak_stages(q_ref, k_ref, v_ref, heads, suffix, first, count, laters, diag_mask):
    tk = suffix.shape[0]
    n_heads = range(len(heads))
    chains = [(b, h) for b in range(count) for h in n_heads]
    index = [first - b for b in range(count)]
    starts = [pl.multiple_of(jnp.maximum(j, 0) * tk, tk) for j in index]

    def masked(c, t):
        b, _ = chains[c]
        if diag_mask is None:
            return t
        return jnp.where(diag_mask if b == 0 else index[b] >= 0, t, 0.0)

    n = range(len(chains))
    qs = [q_ref[seq, :, sl] for seq, sl in heads]
    zq = [_dot_nt(qs[h], k_ref[heads[h][0], pl.ds(starts[b], tk), heads[h][1]]) * (HEAD_DIM ** -0.5)
          for b, h in chains]
    yield
    log_beta = [jnp.minimum(z, 0.0) - jnp.log(1.0 + jnp.exp(-jnp.abs(z))) for z in zq]
    yield
    log_1m = [masked(c, log_beta[c] - zq[c]) for c in n]
    hi = [t.astype(BF16) for t in log_1m]
    lo = [(log_1m[c] - hi[c].astype(F32)).astype(BF16) for c in n]
    yield
    suf = [_dot(hi[c], suffix) + _dot(lo[c], suffix) for c in n]
    yield
    entering, running = {}, list(laters)
    for c, (b, h) in enumerate(chains):
        entering[c] = running[h]
        running[h] = running[h] + suf[c][:, 0:1]
    wgt = [masked(c, jnp.exp(log_beta[c] + (suf[c] - log_1m[c]) + entering[c])).astype(BF16) for c in n]
    yield
    pv = [_dot(wgt[c], v_ref[heads[h][0], pl.ds(starts[b], tk), heads[h][1]])
          for c, (b, h) in enumerate(chains)]
    yield
    contrib = tuple(functools.reduce(lambda x, y: x + y, [pv[c] for c in n if chains[c][1] == h])
                    for h in n_heads)
    largest = jnp.max(functools.reduce(jnp.maximum, running))
    return contrib, tuple(running), largest


def _stickbreak_kernel(q_ref, k_ref, v_ref, o_ref):
    tq = q_ref.shape[1]
    tk = tq
    i = pl.program_id(1)
    heads = [(seq, slice(h * HEAD_DIM, (h + 1) * HEAD_DIM))
             for seq in range(q_ref.shape[0]) for h in range(HB)]
    suffix = jnp.where(lax.broadcasted_iota(jnp.int32, (tk, tk), 0)
                       >= lax.broadcasted_iota(jnp.int32, (tk, tk), 1), 1.0, 0.0).astype(BF16)
    strictly_causal = (lax.broadcasted_iota(jnp.int32, (tq, tk), 1)
                       < lax.broadcasted_iota(jnp.int32, (tq, tk), 0))
    accs, laters, largest = _run_stages(_stickbreak_stages(
        q_ref, k_ref, v_ref, heads, suffix, i, ATT_FIRST_BLOCKS,
        (jnp.zeros((tq, 1), F32),) * len(heads), strictly_causal))

    def more(carry):
        j, largest, _, _ = carry
        return jnp.logical_and(j >= 0, largest > EXP_UNDERFLOW)

    def body(carry):
        j, _, accs, laters = carry
        contrib, laters, largest = _run_stages(_stickbreak_stages(
            q_ref, k_ref, v_ref, heads, suffix, j, 1, laters, None))
        return j - 1, largest, tuple(a + c for a, c in zip(accs, contrib)), laters

    _, _, accs, _ = lax.while_loop(more, body, (i - ATT_FIRST_BLOCKS, largest, accs, laters))
    for (seq, sl), acc in zip(heads, accs):
        o_ref[seq, :, sl] = acc.astype(o_ref.dtype)


def _stickbreak(qkv):
    b, lp, _ = qkv.shape
    tq = ATT_BLOCK
    nseq = ATT_SEQS_PER_STEP
    return pl.pallas_call(
        _stickbreak_kernel,
        grid=(b // nseq, lp // tq),
        in_specs=[pl.BlockSpec((nseq, tq, DB), lambda bi, i: (bi, i, 0)),
                  pl.BlockSpec((nseq, lp, DB), lambda bi, i: (bi, 0, 1)),
                  pl.BlockSpec((nseq, lp, DB), lambda bi, i: (bi, 0, 2))],
        out_specs=pl.BlockSpec((nseq, tq, DB), lambda bi, i: (bi, i, 0)),
        out_shape=jax.ShapeDtypeStruct((b, lp, DB), BF16),
        compiler_params=_params("parallel", "arbitrary"),
        name="stickbreak",
    )(qkv, qkv, qkv)


def _row_subtiles(rows):
    return [slice(r, r + FFN_SUBTILE) for r in range(0, rows, FFN_SUBTILE)]


def _residual_ffn(ms, subs, h_ref, npost_ref, nfpre_ref, wg_ref, wu_ref, wd_ref, nfpost_ref, out_ref):
    h1 = [h_ref[r, :] + _rms(m, npost_ref[...]) for m, r in zip(ms, subs)]
    hn = [_rms(t, nfpre_ref[...]).astype(BF16) for t in h1]
    g = [_dot(t, wg_ref[...]) for t in hn]
    u = [_dot(t, wu_ref[...]) for t in hn]
    a = [(gi * _sigmoid(gi) * ui).astype(BF16) for gi, ui in zip(g, u)]
    f = [_dot(t, wd_ref[...]) for t in a]
    for r, h1i, fi in zip(subs, h1, f):
        out_ref[r, :] = h1i + _rms(fi, nfpost_ref[...])


def _hyb_out_ffn_kernel(oa_ref, ob_ref, wo_ref, *rest):
    subs = _row_subtiles(oa_ref.shape[0])
    ms = [_dot(oa_ref[r, :], wo_ref[:DA, :]) + _dot(ob_ref[r, :], wo_ref[DA:, :]) for r in subs]
    _residual_ffn(ms, subs, *rest)


def _glu_out_ffn_kernel(y_ref, wglu_ref, *rest):
    d = wglu_ref.shape[1] // 2
    subs = _row_subtiles(y_ref.shape[0])
    yv = [_dot(y_ref[r, :], wglu_ref[...]) for r in subs]
    ms = [t[:, :d] * _sigmoid(t[:, d:]) for t in yv]
    _residual_ffn(ms, subs, *rest)


def _mixer_out_ffn(kernel_fn, acts, w_mix, h, npost, nfpre, wg, wu, wd, nfpost, keep=None):
    b, lp, d = h.shape
    dff = wg.shape[1]
    tm = 2 * FFN_SUBTILE
    weights = [_const_spec(w_mix.shape)]
    tail = [_const_spec((1, d)), _const_spec((1, d)), _const_spec((d, dff)), _const_spec((d, dff)),
            _const_spec((dff, d)), _const_spec((1, d))]
    if keep is None:
        rows = b * lp
        acts = [a.reshape(rows, a.shape[2]) for a in acts]
        h = h.reshape(rows, d)
        grid = (rows // tm,)
        row = lambda width: pl.BlockSpec((tm, width), lambda i: (i, 0))
        out_spec, out_shape, sem = row(d), (rows, d), ("parallel",)
    else:
        first, n_rows = keep
        assert first % N_META == 0 and tm % N_META == 0
        grid = (b, n_rows // tm)
        row = lambda width: pl.BlockSpec((None, pl.Element(tm), pl.Element(width)),
                                         lambda bi, i: (bi, pl.multiple_of(first + i * tm, N_META), 0))
        out_spec = pl.BlockSpec((None, tm, d), lambda bi, i: (bi, i, 0))
        out_shape, sem = (b, n_rows, d), ("parallel", "parallel")
    assert grid[-1] * tm == (b * lp if keep is None else keep[1])
    return pl.pallas_call(
        kernel_fn,
        grid=grid,
        in_specs=[row(a.shape[-1]) for a in acts] + weights + [row(d)] + tail,
        out_specs=out_spec,
        out_shape=jax.ShapeDtypeStruct(out_shape, F32),
        compiler_params=_params(*sem),
        name="mixer_out_ffn",
    )(*acts, w_mix, h, npost.reshape(1, d), nfpre.reshape(1, d), wg, wu, wd, nfpost.reshape(1, d))


def _s5_kernel(h_ref, nw_ref, win_ref, bmat_ref, cmat_ref, a_ref, d_ref, y_ref, state_ref, bu_ref,
               u_ref, swap_in_ref, swap_out_ref, hn_ref, hs_ref):
    bsz, tl, dm = h_ref.shape
    width = win_ref.shape[1]
    n_tiles = width // LANE
    pair = 2 * LANE

    @pl.when(pl.program_id(0) == 0)
    def _():
        state_ref[...] = jnp.zeros_like(state_ref)

    hn_ref[...] = _rms(h_ref[...].reshape(bsz * tl, dm), nw_ref[...]).astype(BF16)

    def input_map(k):
        slot = k % 2
        if k % 2 == 0:
            cols = slice(k * LANE, k * LANE + pair)
            hs_ref[:, :, cols] = _dot(hn_ref[...], win_ref[:, cols]).reshape(bsz, tl, pair)
        for b in range(bsz):
            swap_in_ref[slot, b * S5_PITCH:b * S5_PITCH + tl, :] = hs_ref[b, :, k * LANE:(k + 1) * LANE]
        u = jnp.concatenate([swap_in_ref[slot, pl.ds(t, bsz, stride=S5_PITCH), :] for t in range(tl)],
                            axis=0)
        u_ref[slot] = u
        bu_ref[slot] = _dot(u.astype(BF16), bmat_ref[k])

    input_map(0)
    for k in range(n_tiles):
        if k + 1 < n_tiles:
            input_map(k + 1)
        slot = k % 2
        buf = bu_ref.at[slot]
        a_re = jnp.broadcast_to(a_ref[k, 0:1, :], (bsz, S5_HALF))
        a_im = jnp.broadcast_to(a_ref[k, 1:2, :], (bsz, S5_HALF))
        st = state_ref[k]
        x_re, x_im = st[:, :S5_HALF], st[:, S5_HALF:]
        for t in range(tl):
            rs = slice(t * bsz, (t + 1) * bsz)
            bu = buf[rs, :]
            x_re, x_im = (a_re * x_re - a_im * x_im + bu[:, :S5_HALF],
                          a_re * x_im + a_im * x_re + bu[:, S5_HALF:])
            buf[rs, :] = jnp.concatenate([x_re, x_im], axis=1)
        state_ref[k] = jnp.concatenate([x_re, x_im], axis=1)

        lanes = slice(k * LANE, (k + 1) * LANE)
        y = jax.nn.gelu(_dot(buf[...].astype(BF16), cmat_ref[k]) + d_ref[:, lanes] * u_ref[slot])
        for t in range(tl):
            swap_out_ref[slot, pl.ds(t, bsz, stride=S5_PITCH), :] = y[t * bsz:(t + 1) * bsz, :]
        for b in range(bsz):
            y_ref[b, :, lanes] = swap_out_ref[slot, b * S5_PITCH:b * S5_PITCH + tl, :].astype(y_ref.dtype)


def _s5_mixer(h, norm_w, w_in, bmat, cmat, a_bar, d_skip):
    bsz, lp, dm = h.shape
    width = w_in.shape[1]
    tl = S5_TILE
    n_tiles = width // LANE
    assert n_tiles % 2 == 0
    return pl.pallas_call(
        _s5_kernel,
        grid=(lp // tl,),
        in_specs=[pl.BlockSpec((bsz, tl, dm), lambda i: (0, i, 0)),
                  _const_spec((1, dm)), _const_spec(w_in.shape),
                  _const_spec(bmat.shape), _const_spec(cmat.shape),
                  _const_spec(a_bar.shape), _const_spec((1, width))],
        out_specs=pl.BlockSpec((bsz, tl, width), lambda i: (0, i, 0)),
        out_shape=jax.ShapeDtypeStruct((bsz, lp, width), BF16),
        scratch_shapes=[pltpu.VMEM((n_tiles, bsz, 2 * S5_HALF), F32),
                        pltpu.VMEM((2, tl * bsz, 2 * S5_HALF), F32),
                        pltpu.VMEM((2, tl * bsz, LANE), F32),
                        pltpu.VMEM((2, bsz * S5_PITCH, LANE), F32),
                        pltpu.VMEM((2, bsz * S5_PITCH, LANE), F32),
                        pltpu.VMEM((tl * bsz, dm), BF16),
                        pltpu.VMEM((bsz, tl, width), F32)],
        compiler_params=_params("arbitrary"),
        name="s5_mixer",
    )(h, norm_w.reshape(1, dm), w_in, bmat, cmat, a_bar, d_skip.reshape(1, width))


def _s5_discretise_kernel(lre_ref, lim_ref, step_ref, bre_ref, bim_ref,
                          are_ref, aim_ref, bbre_ref, bbim_ref):
    lam_re, lam_im = lre_ref[...], lim_ref[...]
    dt = jnp.exp(step_ref[...])
    mag = jnp.exp(lam_re * dt)
    a_re = mag * jnp.cos(lam_im * dt)
    a_im = mag * jnp.sin(lam_im * dt)
    den = lam_re * lam_re + lam_im * lam_im
    f_re = ((a_re - 1.0) * lam_re + a_im * lam_im) / den
    f_im = (a_im * lam_re - (a_re - 1.0) * lam_im) / den
    are_ref[...] = a_re
    aim_ref[...] = a_im
    bbre_ref[...] = f_re * bre_ref[...] - f_im * bim_ref[...]
    bbim_ref[...] = f_re * bim_ref[...] + f_im * bre_ref[...]


def _s5_discretise(lam_re, lam_im, log_step, b_re, b_im, c_re, c_im):
    n_groups = lam_re.shape[0]
    n_tiles = n_groups // GROUPS_PER_LANE_TILE
    flat = (n_groups, STATE * GROUP)
    spread = lambda t: jnp.repeat(t, GROUP, axis=1)
    a_re, a_im, bb_re, bb_im = pl.pallas_call(
        _s5_discretise_kernel,
        out_shape=[jax.ShapeDtypeStruct(flat, F32)] * 4,
        name="s5_discretise",
    )(spread(lam_re), spread(lam_im), log_step.reshape(n_groups, 1),
      b_re.reshape(flat), b_im.reshape(flat))
    a_re, a_im = a_re[:, ::GROUP], a_im[:, ::GROUP]
    bb_re = bb_re.reshape(n_groups, STATE, GROUP)
    bb_im = bb_im.reshape(n_groups, STATE, GROUP)
    eye = jnp.eye(GROUPS_PER_LANE_TILE, dtype=F32)

    def in_map(bb):
        t = bb.reshape(n_tiles, GROUPS_PER_LANE_TILE, STATE, GROUP)
        t = jnp.einsum("kgpc,gh->kgchp", t, eye)
        return t.reshape(n_tiles, LANE, S5_HALF)

    def out_map(cc):
        t = cc.reshape(n_tiles, GROUPS_PER_LANE_TILE, GROUP, STATE)
        t = jnp.einsum("kgcp,gh->kgphc", t, eye)
        return t.reshape(n_tiles, S5_HALF, LANE)

    bmat = jnp.concatenate([in_map(bb_re), in_map(bb_im)], axis=2).astype(BF16)
    cmat = jnp.concatenate([out_map(c_re), -out_map(c_im)], axis=1).astype(BF16)
    a_bar = jnp.stack([a_re.reshape(n_tiles, S5_HALF), a_im.reshape(n_tiles, S5_HALF)], axis=1)
    return bmat, cmat, a_bar


def kernel(x, meta_tokens, norm_mix_pre, norm_mix_post, norm_ffn_pre, norm_ffn_post, ffn_w_gate,
           ffn_w_up, ffn_w_down, hyb_w_in, hyb_conv_w, dn_a_log, dn_dt_bias, dn_out_norm, hyb_w_out,
           ssm_w_in, ssm_lambda_re, ssm_lambda_im, ssm_log_step, ssm_b_re, ssm_b_im, ssm_c_re,
           ssm_c_im, ssm_d, ssm_w_glu):
    bsz, seq, d = x.shape
    l_real = N_META + seq
    lp = -(-l_real // SEQ_ALIGN) * SEQ_ALIGN
    w_in = hyb_w_in[0]
    n_gate = 2 * HA
    w_cat = jnp.concatenate([
        w_in[:, :4 * DA],
        jnp.pad(w_in[:, 4 * DA:4 * DA + n_gate], ((0, 0), (0, LANE - n_gate))),
        w_in[:, 4 * DA + n_gate:]], axis=1).astype(BF16)
    gate_par = jnp.zeros((2, LANE), F32)
    gate_par = gate_par.at[0, HA:2 * HA].set(dn_a_log[0].astype(F32))
    gate_par = gate_par.at[1, HA:2 * HA].set(dn_dt_bias[0].astype(F32))
    h, qkv_a, zs_a, gates, qkv_b = _hyb_in(x, meta_tokens.astype(x.dtype), lp, norm_mix_pre[0], w_cat,
                                           hyb_conv_w[0].astype(F32), gate_par)
    o_a = _deltanet(qkv_a, zs_a, gates, dn_out_norm[0].reshape(1, HEAD_DIM).astype(F32))
    o_b = _stickbreak(qkv_b)
    h = _mixer_out_ffn(_hyb_out_ffn_kernel, [o_a, o_b], hyb_w_out[0].astype(BF16), h,
                       norm_mix_post[0], norm_ffn_pre[0], ffn_w_gate[0].astype(BF16),
                       ffn_w_up[0].astype(BF16), ffn_w_down[0].astype(BF16), norm_ffn_post[0])

    h = h.reshape(bsz, lp, d)
    bmat, cmat, a_bar = _s5_discretise(ssm_lambda_re[0].astype(F32), ssm_lambda_im[0].astype(F32),
                                       ssm_log_step[0].astype(F32), ssm_b_re[0].astype(F32),
                                       ssm_b_im[0].astype(F32), ssm_c_re[0].astype(F32),
                                       ssm_c_im[0].astype(F32))
    y = _s5_mixer(h, norm_mix_pre[1], ssm_w_in[0].astype(BF16), bmat, cmat, a_bar,
                  ssm_d[0].astype(F32))
    return _mixer_out_ffn(_glu_out_ffn_kernel, [y], ssm_w_glu[0].astype(BF16),
                          h, norm_mix_post[1], norm_ffn_pre[1],
                          ffn_w_gate[1].astype(BF16), ffn_w_up[1].astype(BF16),
                          ffn_w_down[1].astype(BF16), norm_ffn_post[1], keep=(N_META, seq))
```

```python
import functools

import jax
import jax.numpy as jnp
from jax import lax
from jax.experimental import pallas as pl
from jax.experimental.pallas import tpu as pltpu

N_META = 16
HEAD_DIM = 128
HA = 4
HB = 4
DA = HA * HEAD_DIM
DB = HB * HEAD_DIM
CONV_WIDTH = 4
GROUP = 16
STATE = 64
EPS = 1e-6

LANE = 128
SEQ_ALIGN = 128
DN_CHUNK = 128
DN_SEQS_PER_STEP = 4
INV_BASE = 16
ATT_BLOCK = 128
ATT_SEQS_PER_STEP = 2
ATT_FIRST_BLOCKS = 3
FFN_SUBTILE = 256
S5_TILE = 64
S5_PITCH = S5_TILE + 8
GROUPS_PER_LANE_TILE = LANE // GROUP
S5_HALF = GROUPS_PER_LANE_TILE * STATE
VMEM_LIMIT = 56 * 1024 * 1024
EXP_UNDERFLOW = -105.0

F32 = jnp.float32
BF16 = jnp.bfloat16


def _dot(a, b):
    return jnp.dot(a, b, preferred_element_type=F32)


def _dot_nt(a, b):
    return lax.dot_general(a, b, (((1,), (1,)), ((), ())), preferred_element_type=F32)


def _dot_tn(a, b):
    return lax.dot_general(a, b, (((0,), (0,)), ((), ())), preferred_element_type=F32)


def _rms(x, w):
    return x * lax.rsqrt(jnp.mean(x * x, axis=-1, keepdims=True) + EPS) * w


def _sigmoid(x):
    return 1.0 / (1.0 + jnp.exp(-x))


def _softplus(x):
    return jnp.maximum(x, 0.0) + jnp.log(1.0 + jnp.exp(-jnp.abs(x)))


def _const_spec(shape):
    zeros = (0,) * len(shape)
    return pl.BlockSpec(shape, lambda *_: zeros, pipeline_mode=pl.Buffered(1))


def _params(*sem):
    return pltpu.CompilerParams(dimension_semantics=sem, vmem_limit_bytes=VMEM_LIMIT)


HYB_COLS = (("qkv_a", 0, 3 * DA), ("z_a", 3 * DA, DA), ("gates", 4 * DA, LANE),
            ("qkv_b", 4 * DA + LANE, 3 * DB))
HALO = 8


def _sequence_tile(x_ref, meta_ref, tile, n_tiles, n_pad):
    xb = x_ref[...]
    tm = xb.shape[0]
    first = jnp.concatenate([meta_ref[...], xb[:tm - N_META]], axis=0)
    if n_pad:
        last = jnp.concatenate([xb[n_pad:], jnp.zeros((n_pad, xb.shape[1]), xb.dtype)], axis=0)
    else:
        last = xb
    return jnp.where(tile == 0, first, jnp.where(tile == n_tiles - 1, last, xb))


def _hyb_in_kernel(x_ref, meta_ref, nw_ref, w_ref, convw_ref, gpar_ref,
                   h_ref, qkv_ref, zs_ref, gate_ref, qkvb_ref, tail_ref, *, n_pad):
    cols = {name: slice(start, start + width) for name, start, width in HYB_COLS}

    @pl.when(pl.program_id(1) == 0)
    def _():
        tail_ref[...] = jnp.zeros_like(tail_ref)

    h0 = _sequence_tile(x_ref, meta_ref, pl.program_id(1), pl.num_programs(1), n_pad)
    h_ref[0] = h0
    y = _rms(h0, nw_ref[...]).astype(BF16)
    first_rows = lax.broadcasted_iota(jnp.int32, (HALO, DA), 0)

    def conv_silu(pre, cs):
        tail = tail_ref[:, cs]
        acc = pre * convw_ref[CONV_WIDTH - 1:CONV_WIDTH, cs]
        for s in range(1, CONV_WIDTH):
            rolled = pltpu.roll(pre, s, 0)
            head = jnp.where(first_rows < s, pltpu.roll(tail, s, 0), rolled[:HALO])
            shifted = jnp.concatenate([head, rolled[HALO:]], axis=0)
            acc = acc + shifted * convw_ref[CONV_WIDTH - 1 - s:CONV_WIDTH - s, cs]
        tail_ref[:, cs] = pre[pre.shape[0] - HALO:, :]
        return acc * _sigmoid(acc)

    def unit_heads(xc, base, gain):
        for h in range(HA):
            t = xc[:, h * HEAD_DIM:(h + 1) * HEAD_DIM]
            t = t * (lax.rsqrt(jnp.sum(t * t, axis=-1, keepdims=True) + EPS) * gain)
            qkv_ref[0, :, base + h * HEAD_DIM:base + (h + 1) * HEAD_DIM] = t.astype(qkv_ref.dtype)

    def post_q(pre):
        unit_heads(conv_silu(pre, slice(0, DA)), 0, HEAD_DIM ** -0.5)

    def post_k(pre):
        unit_heads(conv_silu(pre, slice(DA, 2 * DA)), DA, 1.0)

    def post_v(pre):
        qkv_ref[0, :, 2 * DA:] = conv_silu(pre, slice(2 * DA, 3 * DA)).astype(qkv_ref.dtype)

    def post_z(zlin):
        zs_ref[0] = (zlin * _sigmoid(zlin)).astype(zs_ref.dtype)

    def post_gates(ba):
        g = -jnp.exp(gpar_ref[0:1, :]) * _softplus(ba + gpar_ref[1:2, :])
        gate_ref[0] = jnp.where(lax.broadcasted_iota(jnp.int32, ba.shape, 1) < HA, _sigmoid(ba), g)

    def post_b(part):
        def store(pre):
            qkvb_ref[0, :, part * DB:(part + 1) * DB] = pre.astype(qkvb_ref.dtype)
        return store

    qa, qb = cols["qkv_a"].start, cols["qkv_b"].start
    attn = [(slice(qb + part * DB, qb + (part + 1) * DB), post_b(part)) for part in range(3)]
    stages = [(slice(qa, qa + DA), post_q), attn[0], (slice(qa + DA, qa + 2 * DA), post_k), attn[1],
              (slice(qa + 2 * DA, qa + 3 * DA), post_v), attn[2], (cols["z_a"], post_z),
              (cols["gates"], post_gates)]
    pending = _dot(y, w_ref[:, stages[0][0]])
    for idx, (_, post) in enumerate(stages):
        upcoming = _dot(y, w_ref[:, stages[idx + 1][0]]) if idx + 1 < len(stages) else None
        post(pending)
        pending = upcoming


def _hyb_in(x, meta, lp, norm_w, w, conv_w, gate_par):
    b, seq, d = x.shape
    n_pad = lp - (N_META + seq)
    tm = next(rows for rows in (384, 256, 128) if lp % rows == 0)
    assert lp // tm >= 2 and seq >= tm and n_pad % 8 == 0 and n_pad <= tm
    row = lambda width: pl.BlockSpec((1, tm, width), lambda bi, i: (bi, i, 0))
    n_tiles = lp // tm

    def x_rows(bi, i):
        start = (i * tm - N_META + N_META * (i == 0).astype(jnp.int32)
                 - n_pad * (i == n_tiles - 1).astype(jnp.int32))
        return bi, pl.multiple_of(start, HALO), 0

    x_spec = pl.BlockSpec((None, pl.Element(tm), pl.Element(d)), x_rows)
    widths = {name: width for name, _, width in HYB_COLS}
    return pl.pallas_call(
        functools.partial(_hyb_in_kernel, n_pad=n_pad),
        grid=(b, lp // tm),
        in_specs=[x_spec, _const_spec((N_META, d)), _const_spec((1, d)), _const_spec(w.shape),
                  _const_spec(conv_w.shape), _const_spec(gate_par.shape)],
        out_specs=[row(d), row(widths["qkv_a"]), row(widths["z_a"]), row(widths["gates"]),
                   row(widths["qkv_b"])],
        out_shape=[jax.ShapeDtypeStruct((b, lp, d), F32),
                   jax.ShapeDtypeStruct((b, lp, widths["qkv_a"]), BF16),
                   jax.ShapeDtypeStruct((b, lp, widths["z_a"]), BF16),
                   jax.ShapeDtypeStruct((b, lp, widths["gates"]), F32),
                   jax.ShapeDtypeStruct((b, lp, widths["qkv_b"]), BF16)],
        scratch_shapes=[pltpu.VMEM((HALO, widths["qkv_a"]), F32)],
        compiler_params=_params("parallel", "arbitrary"),
        name="hyb_in",
    )(x, meta, norm_w.reshape(1, d), w, conv_w, gate_par)


def _run_stages(stage_generator):
    try:
        while True:
            next(stage_generator)
    except StopIteration as stop:
        return stop.value


def _unit_lower_inverses(lowers, ri, ci):
    c = lowers[0].shape[0]

    def same_block(size):
        shift = size.bit_length() - 1
        return (ri >> shift) == (ci >> shift)

    def mm(a, b):
        return _dot(a.astype(BF16), b.astype(BF16))

    eye = jnp.where(ci == ri, 1.0, 0.0).astype(F32)
    in_base = same_block(INV_BASE)
    xs = [jnp.where(in_base, -low, 0.0) for low in lowers]
    ps = [eye + x for x in xs]
    xs = [mm(x, x) for x in xs]
    yield
    covered = 2
    while 2 * covered < INV_BASE:
        both = [mm(jnp.concatenate([x, p], axis=0), x) for x, p in zip(xs, ps)]
        yield
        xs = [t[:c] for t in both]
        ps = [p + t[c:] for p, t in zip(ps, both)]
        covered *= 2
    ps = [p + mm(p, x) for p, x in zip(ps, xs)]
    yield
    size = INV_BASE
    while size < c:
        joins = jnp.logical_and(same_block(2 * size), jnp.logical_not(same_block(size)))
        ys = [mm(jnp.where(joins, low, 0.0), p) for low, p in zip(lowers, ps)]
        yield
        ps = [p - mm(p, y) for p, y in zip(ps, ys)]
        yield
        size *= 2
    return ps


def _deltanet_stages(qkv_ref, zs_ref, gate_ref, onorm_ref, o_ref, state_ref):
    nseq, c = qkv_ref.shape[0], qkv_ref.shape[1]
    units = [(s, h) for s in range(nseq) for h in range(HA)]
    n = range(len(units))

    @pl.when(pl.program_id(1) == 0)
    def _():
        state_ref[...] = jnp.zeros_like(state_ref)

    ri = lax.broadcasted_iota(jnp.int32, (c, c), 0)
    ci = lax.broadcasted_iota(jnp.int32, (c, c), 1)
    incl = ci <= ri
    strict = ci < ri
    tril = jnp.where(incl, 1.0, 0.0).astype(BF16)
    sel = jnp.where(lax.broadcasted_iota(jnp.int32, (8, LANE), 1)
                    == lax.broadcasted_iota(jnp.int32, (8, LANE), 0) + HA, 1.0, 0.0).astype(BF16)

    def exact_terms(t):
        hi = t.astype(BF16)
        rest = t - hi.astype(F32)
        mid = rest.astype(BF16)
        return hi, mid, (rest - mid.astype(F32)).astype(BF16)

    gates = [gate_ref[s] for s in range(nseq)]
    g_cum = [sum(_dot(tril, part) for part in exact_terms(g)) for g in gates]
    yield
    g_rows = [sum(_dot_nt(sel, part) for part in exact_terms(g)) for g in g_cum]
    yield

    def head_cols(base, s, h):
        return qkv_ref[s, :, base + h * HEAD_DIM:base + (h + 1) * HEAD_DIM]

    q_bf = [head_cols(0, s, h) for s, h in units]
    k_bf = [head_cols(DA, s, h) for s, h in units]
    q = [t.astype(F32) for t in q_bf]
    k = [t.astype(F32) for t in k_bf]
    v = [head_cols(2 * DA, s, h).astype(F32) for s, h in units]
    beta = [gates[s][:, h:h + 1] for s, h in units]
    g_col = [g_cum[s][:, HA + h:HA + h + 1] for s, h in units]
    g_row = [g_rows[s][h:h + 1, :] for s, h in units]
    g_last = [g_cum[s][c - 1:c, HA + h:HA + h + 1] for s, h in units]
    decay = [jnp.where(incl, jnp.exp(jnp.where(incl, g_col[u] - g_row[u], 0.0)), 0.0) for u in n]
    exp_g = [jnp.exp(g) for g in g_col]
    kb = [k[u] * beta[u] for u in n]
    yield
    scores = [_dot_nt(jnp.concatenate([kb[u].astype(BF16), q_bf[u]], axis=0), k_bf[u]) for u in n]
    yield
    lower = [jnp.where(strict, scores[u][:c] * decay[u], 0.0) for u in n]
    qk = [jnp.where(incl, scores[u][c:] * decay[u], 0.0).astype(BF16) for u in n]
    yield
    t_inv = yield from _unit_lower_inverses(lower, ri, ci)
    rhs = [jnp.concatenate([v[u] * beta[u], kb[u] * exp_g[u]], axis=1).astype(BF16) for u in n]
    sol = [_dot(t_inv[u].astype(BF16), rhs[u]) for u in n]
    yield
    q_dec = [(q[u] * exp_g[u]).astype(BF16) for u in n]
    k_dec = [(k[u] * jnp.exp(g_last[u] - g_col[u])).astype(BF16) for u in n]

    state = [state_ref[u] for u in n]
    state_bf = [s.astype(BF16) for s in state]
    from_state = [_dot(jnp.concatenate([sol[u][:, HEAD_DIM:].astype(BF16), q_dec[u]], axis=0), state_bf[u])
                  for u in n]
    yield
    v_new = [(sol[u][:, :HEAD_DIM] - from_state[u][:c]).astype(BF16) for u in n]
    o = [from_state[u][c:] + _dot(qk[u], v_new[u]) for u in n]
    yield
    for u in n:
        state_ref[u] = state[u] * jnp.exp(g_last[u]) + _dot_tn(k_dec[u], v_new[u])
    yield

    onorm = onorm_ref[...]
    for u, (s, h) in enumerate(units):
        sl = slice(h * HEAD_DIM, (h + 1) * HEAD_DIM)
        o_ref[s, :, sl] = (_rms(o[u], onorm) * zs_ref[s, :, sl].astype(F32)).astype(o_ref.dtype)


def _deltanet_kernel(*refs):
    _run_stages(_deltanet_stages(*refs))


def _deltanet(qkv, zs, gates, out_norm):
    b, lp, _ = qkv.shape
    c = DN_CHUNK
    nseq = DN_SEQS_PER_STEP
    return pl.pallas_call(
        _deltanet_kernel,
        grid=(b // nseq, lp // c),
        in_specs=[pl.BlockSpec((nseq, c, 3 * DA), lambda bi, i: (bi, i, 0)),
                  pl.BlockSpec((nseq, c, DA), lambda bi, i: (bi, i, 0)),
                  pl.BlockSpec((nseq, c, LANE), lambda bi, i: (bi, i, 0)),
                  _const_spec((1, HEAD_DIM))],
        out_specs=pl.BlockSpec((nseq, c, DA), lambda bi, i: (bi, i, 0)),
        out_shape=jax.ShapeDtypeStruct((b, lp, DA), BF16),
        scratch_shapes=[pltpu.VMEM((nseq * HA, HEAD_DIM, HEAD_DIM), F32)],
        compiler_params=_params("parallel", "arbitrary"),
        name="deltanet",
    )(qkv, zs, gates, out_norm)


def _stickbreak_stages(q_ref, k_ref, v_ref, heads, suffix, first, count, laters, diag_mask):
    tk = suffix.shape[0]
    n_heads = range(len(heads))
    chains = [(b, h) for b in range(count) for h in n_heads]
    index = [first - b for b in range(count)]
    starts = [pl.multiple_of(jnp.maximum(j, 0) * tk, tk) for j in index]

    def masked(c, t):
        b, _ = chains[c]
        if diag_mask is None:
            return t
        return jnp.where(diag_mask if b == 0 else index[b] >= 0, t, 0.0)

    n = range(len(chains))
    qs = [q_ref[seq, :, sl] for seq, sl in heads]
    zq = [_dot_nt(qs[h], k_ref[heads[h][0], pl.ds(starts[b], tk), heads[h][1]]) * (HEAD_DIM ** -0.5)
          for b, h in chains]
    yield
    log_beta = [jnp.minimum(z, 0.0) - jnp.log(1.0 + jnp.exp(-jnp.abs(z))) for z in zq]
    yield
    log_1m = [masked(c, log_beta[c] - zq[c]) for c in n]
    hi = [t.astype(BF16) for t in log_1m]
    lo = [(log_1m[c] - hi[c].astype(F32)).astype(BF16) for c in n]
    yield
    suf = [_dot(hi[c], suffix) + _dot(lo[c], suffix) for c in n]
    yield
    entering, running = {}, list(laters)
    for c, (b, h) in enumerate(chains):
        entering[c] = running[h]
        running[h] = running[h] + suf[c][:, 0:1]
    wgt = [masked(c, jnp.exp(log_beta[c] + (suf[c] - log_1m[c]) + entering[c])).astype(BF16) for c in n]
    yield
    pv = [_dot(wgt[c], v_ref[heads[h][0], pl.ds(starts[b], tk), heads[h][1]])
          for c, (b, h) in enumerate(chains)]
    yield
    contrib = tuple(functools.reduce(lambda x, y: x + y, [pv[c] for c in n if chains[c][1] == h])
                    for h in n_heads)
    largest = jnp.max(functools.reduce(jnp.maximum, running))
    return contrib, tuple(running), largest


def _stickbreak_kernel(q_ref, k_ref, v_ref, o_ref):
    tq = q_ref.shape[1]
    tk = tq
    i = pl.program_id(1)
    heads = [(seq, slice(h * HEAD_DIM, (h + 1) * HEAD_DIM))
             for seq in range(q_ref.shape[0]) for h in range(HB)]
    suffix = jnp.where(lax.broadcasted_iota(jnp.int32, (tk, tk), 0)
                       >= lax.broadcasted_iota(jnp.int32, (tk, tk), 1), 1.0, 0.0).astype(BF16)
    strictly_causal = (lax.broadcasted_iota(jnp.int32, (tq, tk), 1)
                       < lax.broadcasted_iota(jnp.int32, (tq, tk), 0))
    accs, laters, largest = _run_stages(_stickbreak_stages(
        q_ref, k_ref, v_ref, heads, suffix, i, ATT_FIRST_BLOCKS,
        (jnp.zeros((tq, 1), F32),) * len(heads), strictly_causal))

    def more(carry):
        j, largest, _, _ = carry
        return jnp.logical_and(j >= 0, largest > EXP_UNDERFLOW)

    def body(carry):
        j, _, accs, laters = carry
        contrib, laters, largest = _run_stages(_stickbreak_stages(
            q_ref, k_ref, v_ref, heads, suffix, j, 1, laters, None))
        return j - 1, largest, tuple(a + c for a, c in zip(accs, contrib)), laters

    _, _, accs, _ = lax.while_loop(more, body, (i - ATT_FIRST_BLOCKS, largest, accs, laters))
    for (seq, sl), acc in zip(heads, accs):
        o_ref[seq, :, sl] = acc.astype(o_ref.dtype)


def _stickbreak(qkv):
    b, lp, _ = qkv.shape
    tq = ATT_BLOCK
    nseq = ATT_SEQS_PER_STEP
    return pl.pallas_call(
        _stickbreak_kernel,
        grid=(b // nseq, lp // tq),
        in_specs=[pl.BlockSpec((nseq, tq, DB), lambda bi, i: (bi, i, 0)),
                  pl.BlockSpec((nseq, lp, DB), lambda bi, i: (bi, 0, 1)),
                  pl.BlockSpec((nseq, lp, DB), lambda bi, i: (bi, 0, 2))],
        out_specs=pl.BlockSpec((nseq, tq, DB), lambda bi, i: (bi, i, 0)),
        out_shape=jax.ShapeDtypeStruct((b, lp, DB), BF16),
        compiler_params=_params("parallel", "arbitrary"),
        name="stickbreak",
    )(qkv, qkv, qkv)


def _row_subtiles(rows):
    return [slice(r, r + FFN_SUBTILE) for r in range(0, rows, FFN_SUBTILE)]


def _residual_ffn(ms, subs, h_ref, npost_ref, nfpre_ref, wg_ref, wu_ref, wd_ref, nfpost_ref, out_ref):
    h1 = [h_ref[r, :] + _rms(m, npost_ref[...]) for m, r in zip(ms, subs)]
    hn = [_rms(t, nfpre_ref[...]).astype(BF16) for t in h1]
    g = [_dot(t, wg_ref[...]) for t in hn]
    u = [_dot(t, wu_ref[...]) for t in hn]
    a = [(gi * _sigmoid(gi) * ui).astype(BF16) for gi, ui in zip(g, u)]
    f = [_dot(t, wd_ref[...]) for t in a]
    for r, h1i, fi in zip(subs, h1, f):
        out_ref[r, :] = h1i + _rms(fi, nfpost_ref[...])


def _hyb_out_ffn_kernel(oa_ref, ob_ref, wo_ref, *rest):
    subs = _row_subtiles(oa_ref.shape[0])
    ms = [_dot(oa_ref[r, :], wo_ref[:DA, :]) + _dot(ob_ref[r, :], wo_ref[DA:, :]) for r in subs]
    _residual_ffn(ms, subs, *rest)


def _glu_out_ffn_kernel(y_ref, wglu_ref, *rest):
    d = wglu_ref.shape[1] // 2
    subs = _row_subtiles(y_ref.shape[0])
    yv = [_dot(y_ref[r, :], wglu_ref[...]) for r in subs]
    ms = [t[:, :d] * _sigmoid(t[:, d:]) for t in yv]
    _residual_ffn(ms, subs, *rest)


def _mixer_out_ffn(kernel_fn, acts, w_mix, h, npost, nfpre, wg, wu, wd, nfpost, keep=None):
    b, lp, d = h.shape
    dff = wg.shape[1]
    tm = 2 * FFN_SUBTILE
    weights = [_const_spec(w_mix.shape)]
    tail = [_const_spec((1, d)), _const_spec((1, d)), _const_spec((d, dff)), _const_spec((d, dff)),
            _const_spec((dff, d)), _const_spec((1, d))]
    if keep is None:
        rows = b * lp
        acts = [a.reshape(rows, a.shape[2]) for a in acts]
        h = h.reshape(rows, d)
        grid = (rows // tm,)
        row = lambda width: pl.BlockSpec((tm, width), lambda i: (i, 0))
        out_spec, out_shape, sem = row(d), (rows, d), ("parallel",)
    else:
        first, n_rows = keep
        assert first % N_META == 0 and tm % N_META == 0
        grid = (b, n_rows // tm)
        row = lambda width: pl.BlockSpec((None, pl.Element(tm), pl.Element(width)),
                                         lambda bi, i: (bi, pl.multiple_of(first + i * tm, N_META), 0))
        out_spec = pl.BlockSpec((None, tm, d), lambda bi, i: (bi, i, 0))
        out_shape, sem = (b, n_rows, d), ("parallel", "parallel")
    assert grid[-1] * tm == (b * lp if keep is None else keep[1])
    return pl.pallas_call(
        kernel_fn,
        grid=grid,
        in_specs=[row(a.shape[-1]) for a in acts] + weights + [row(d)] + tail,
        out_specs=out_spec,
        out_shape=jax.ShapeDtypeStruct(out_shape, F32),
        compiler_params=_params(*sem),
        name="mixer_out_ffn",
    )(*acts, w_mix, h, npost.reshape(1, d), nfpre.reshape(1, d), wg, wu, wd, nfpost.reshape(1, d))


def _s5_kernel(h_ref, nw_ref, win_ref, bmat_ref, cmat_ref, a_ref, d_ref, y_ref, state_ref, bu_ref,
               u_ref, swap_in_ref, swap_out_ref, hn_ref, hs_ref):
    bsz, tl, dm = h_ref.shape
    width = win_ref.shape[1]
    n_tiles = width // LANE
    pair = 2 * LANE

    @pl.when(pl.program_id(0) == 0)
    def _():
        state_ref[...] = jnp.zeros_like(state_ref)

    hn_ref[...] = _rms(h_ref[...].reshape(bsz * tl, dm), nw_ref[...]).astype(BF16)

    def input_map(k):
        slot = k % 2
        if k % 2 == 0:
            cols = slice(k * LANE, k * LANE + pair)
            hs_ref[:, :, cols] = _dot(hn_ref[...], win_ref[:, cols]).reshape(bsz, tl, pair)
        for b in range(bsz):
            swap_in_ref[slot, b * S5_PITCH:b * S5_PITCH + tl, :] = hs_ref[b, :, k * LANE:(k + 1) * LANE]
        u = jnp.concatenate([swap_in_ref[slot, pl.ds(t, bsz, stride=S5_PITCH), :] for t in range(tl)],
                            axis=0)
        u_ref[slot] = u
        bu_ref[slot] = _dot(u.astype(BF16), bmat_ref[k])

    input_map(0)
    for k in range(n_tiles):
        if k + 1 < n_tiles:
            input_map(k + 1)
        slot = k % 2
        buf = bu_ref.at[slot]
        a_re = jnp.broadcast_to(a_ref[k, 0:1, :], (bsz, S5_HALF))
        a_im = jnp.broadcast_to(a_ref[k, 1:2, :], (bsz, S5_HALF))
        st = state_ref[k]
        x_re, x_im = st[:, :S5_HALF], st[:, S5_HALF:]
        for t in range(tl):
            rs = slice(t * bsz, (t + 1) * bsz)
            bu = buf[rs, :]
            x_re, x_im = (a_re * x_re - a_im * x_im + bu[:, :S5_HALF],
                          a_re * x_im + a_im * x_re + bu[:, S5_HALF:])
            buf[rs, :] = jnp.concatenate([x_re, x_im], axis=1)
        state_ref[k] = jnp.concatenate([x_re, x_im], axis=1)

        lanes = slice(k * LANE, (k + 1) * LANE)
        y = jax.nn.gelu(_dot(buf[...].astype(BF16), cmat_ref[k]) + d_ref[:, lanes] * u_ref[slot])
        for t in range(tl):
            swap_out_ref[slot, pl.ds(t, bsz, stride=S5_PITCH), :] = y[t * bsz:(t + 1) * bsz, :]
        for b in range(bsz):
            y_ref[b, :, lanes] = swap_out_ref[slot, b * S5_PITCH:b * S5_PITCH + tl, :].astype(y_ref.dtype)


def _s5_mixer(h, norm_w, w_in, bmat, cmat, a_bar, d_skip):
    bsz, lp, dm = h.shape
    width = w_in.shape[1]
    tl = S5_TILE
    n_tiles = width // LANE
    assert n_tiles % 2 == 0
    return pl.pallas_call(
        _s5_kernel,
        grid=(lp // tl,),
        in_specs=[pl.BlockSpec((bsz, tl, dm), lambda i: (0, i, 0)),
                  _const_spec((1, dm)), _const_spec(w_in.shape),
                  _const_spec(bmat.shape), _const_spec(cmat.shape),
                  _const_spec(a_bar.shape), _const_spec((1, width))],
        out_specs=pl.BlockSpec((bsz, tl, width), lambda i: (0, i, 0)),
        out_shape=jax.ShapeDtypeStruct((bsz, lp, width), BF16),
        scratch_shapes=[pltpu.VMEM((n_tiles, bsz, 2 * S5_HALF), F32),
                        pltpu.VMEM((2, tl * bsz, 2 * S5_HALF), F32),
                        pltpu.VMEM((2, tl * bsz, LANE), F32),
                        pltpu.VMEM((2, bsz * S5_PITCH, LANE), F32),
                        pltpu.VMEM((2, bsz * S5_PITCH, LANE), F32),
                        pltpu.VMEM((tl * bsz, dm), BF16),
                        pltpu.VMEM((bsz, tl, width), F32)],
        compiler_params=_params("arbitrary"),
        name="s5_mixer",
    )(h, norm_w.reshape(1, dm), w_in, bmat, cmat, a_bar, d_skip.reshape(1, width))


def _s5_discretise_kernel(lre_ref, lim_ref, step_ref, bre_ref, bim_ref,
                          are_ref, aim_ref, bbre_ref, bbim_ref):
    lam_re, lam_im = lre_ref[...], lim_ref[...]
    dt = jnp.exp(step_ref[...])
    mag = jnp.exp(lam_re * dt)
    a_re = mag * jnp.cos(lam_im * dt)
    a_im = mag * jnp.sin(lam_im * dt)
    den = lam_re * lam_re + lam_im * lam_im
    f_re = ((a_re - 1.0) * lam_re + a_im * lam_im) / den
    f_im = (a_im * lam_re - (a_re - 1.0) * lam_im) / den
    are_ref[...] = a_re
    aim_ref[...] = a_im
    bbre_ref[...] = f_re * bre_ref[...] - f_im * bim_ref[...]
    bbim_ref[...] = f_re * bim_ref[...] + f_im * bre_ref[...]


def _s5_discretise(lam_re, lam_im, log_step, b_re, b_im, c_re, c_im):
    n_groups = lam_re.shape[0]
    n_tiles = n_groups // GROUPS_PER_LANE_TILE
    flat = (n_groups, STATE * GROUP)
    spread = lambda t: jnp.repeat(t, GROUP, axis=1)
    a_re, a_im, bb_re, bb_im = pl.pallas_call(
        _s5_discretise_kernel,
        out_shape=[jax.ShapeDtypeStruct(flat, F32)] * 4,
        name="s5_discretise",
    )(spread(lam_re), spread(lam_im), log_step.reshape(n_groups, 1),
      b_re.reshape(flat), b_im.reshape(flat))
    a_re, a_im = a_re[:, ::GROUP], a_im[:, ::GROUP]
    bb_re = bb_re.reshape(n_groups, STATE, GROUP)
    bb_im = bb_im.reshape(n_groups, STATE, GROUP)
    eye = jnp.eye(GROUPS_PER_LANE_TILE, dtype=F32)

    def in_map(bb):
        t = bb.reshape(n_tiles, GROUPS_PER_LANE_TILE, STATE, GROUP)
        t = jnp.einsum("kgpc,gh->kgchp", t, eye)
        return t.reshape(n_tiles, LANE, S5_HALF)

    def out_map(cc):
        t = cc.reshape(n_tiles, GROUPS_PER_LANE_TILE, GROUP, STATE)
        t = jnp.einsum("kgcp,gh->kgphc", t, eye)
        return t.reshape(n_tiles, S5_HALF, LANE)

    bmat = jnp.concatenate([in_map(bb_re), in_map(bb_im)], axis=2).astype(BF16)
    cmat = jnp.concatenate([out_map(c_re), -out_map(c_im)], axis=1).astype(BF16)
    a_bar = jnp.stack([a_re.reshape(n_tiles, S5_HALF), a_im.reshape(n_tiles, S5_HALF)], axis=1)
    return bmat, cmat, a_bar


def kernel(x, meta_tokens, norm_mix_pre, norm_mix_post, norm_ffn_pre, norm_ffn_post, ffn_w_gate,
           ffn_w_up, ffn_w_down, hyb_w_in, hyb_conv_w, dn_a_log, dn_dt_bias, dn_out_norm, hyb_w_out,
           ssm_w_in, ssm_lambda_re, ssm_lambda_im, ssm_log_step, ssm_b_re, ssm_b_im, ssm_c_re,
           ssm_c_im, ssm_d, ssm_w_glu):
    bsz, seq, d = x.shape
    l_real = N_META + seq
    lp = -(-l_real // SEQ_ALIGN) * SEQ_ALIGN
    w_in = hyb_w_in[0]
    n_gate = 2 * HA
    w_cat = jnp.concatenate([
        w_in[:, :4 * DA],
        jnp.pad(w_in[:, 4 * DA:4 * DA + n_gate], ((0, 0), (0, LANE - n_gate))),
        w_in[:, 4 * DA + n_gate:]], axis=1).astype(BF16)
    gate_par = jnp.zeros((2, LANE), F32)
    gate_par = gate_par.at[0, HA:2 * HA].set(dn_a_log[0].astype(F32))
    gate_par = gate_par.at[1, HA:2 * HA].set(dn_dt_bias[0].astype(F32))
    h, qkv_a, zs_a, gates, qkv_b = _hyb_in(x, meta_tokens.astype(x.dtype), lp, norm_mix_pre[0], w_cat,
                                           hyb_conv_w[0].astype(F32), gate_par)
    o_a = _deltanet(qkv_a, zs_a, gates, dn_out_norm[0].reshape(1, HEAD_DIM).astype(F32))
    o_b = _stickbreak(qkv_b)
    h = _mixer_out_ffn(_hyb_out_ffn_kernel, [o_a, o_b], hyb_w_out[0].astype(BF16), h,
                       norm_mix_post[0], norm_ffn_pre[0], ffn_w_gate[0].astype(BF16),
                       ffn_w_up[0].astype(BF16), ffn_w_down[0].astype(BF16), norm_ffn_post[0])

    h = h.reshape(bsz, lp, d)
    bmat, cmat, a_bar = _s5_discretise(ssm_lambda_re[0].astype(F32), ssm_lambda_im[0].astype(F32),
                                       ssm_log_step[0].astype(F32), ssm_b_re[0].astype(F32),
                                       ssm_b_im[0].astype(F32), ssm_c_re[0].astype(F32),
                                       ssm_c_im[0].astype(F32))
    y = _s5_mixer(h, norm_mix_pre[1], ssm_w_in[0].astype(BF16), bmat, cmat, a_bar,
                  ssm_d[0].astype(F32))
    return _mixer_out_ffn(_glu_out_ffn_kernel, [y], ssm_w_glu[0].astype(BF16),
                          h, norm_mix_post[1], norm_ffn_pre[1],
                          ffn_w_gate[1].astype(BF16), ffn_w_up[1].astype(BF16),
                          ffn_w_down[1].astype(BF16), norm_ffn_post[1], keep=(N_META, seq))
```

```python
import functools

import jax
import jax.numpy as jnp
from jax import lax
from jax.experimental import pallas as pl
from jax.experimental.pallas import tpu as pltpu

N_META = 16
HEAD_DIM = 128
HA = 4
HB = 4
DA = HA * HEAD_DIM
DB = HB * HEAD_DIM
CONV_WIDTH = 4
GROUP = 16
STATE = 64
EPS = 1e-6

LANE = 128
SEQ_ALIGN = 128
DN_CHUNK = 128
DN_SEQS_PER_STEP = 4
INV_BASE = 16
ATT_BLOCK = 128
ATT_SEQS_PER_STEP = 2
ATT_FIRST_BLOCKS = 3
FFN_SUBTILE = 256
S5_TILE = 64
S5_PITCH = S5_TILE + 8
GROUPS_PER_LANE_TILE = LANE // GROUP
S5_HALF = GROUPS_PER_LANE_TILE * STATE
VMEM_LIMIT = 56 * 1024 * 1024
EXP_UNDERFLOW = -105.0

F32 = jnp.float32
BF16 = jnp.bfloat16


def _dot(a, b):
    return jnp.dot(a, b, preferred_element_type=F32)


def _dot_nt(a, b):
    return lax.dot_general(a, b, (((1,), (1,)), ((), ())), preferred_element_type=F32)


def _dot_tn(a, b):
    return lax.dot_general(a, b, (((0,), (0,)), ((), ())), preferred_element_type=F32)


def _rms(x, w):
    return x * lax.rsqrt(jnp.mean(x * x, axis=-1, keepdims=True) + EPS) * w


def _sigmoid(x):
    return 1.0 / (1.0 + jnp.exp(-x))


def _softplus(x):
    return jnp.maximum(x, 0.0) + jnp.log(1.0 + jnp.exp(-jnp.abs(x)))


def _const_spec(shape):
    zeros = (0,) * len(shape)
    return pl.BlockSpec(shape, lambda *_: zeros, pipeline_mode=pl.Buffered(1))


def _params(*sem):
    return pltpu.CompilerParams(dimension_semantics=sem, vmem_limit_bytes=VMEM_LIMIT)


HYB_COLS = (("qkv_a", 0, 3 * DA), ("z_a", 3 * DA, DA), ("gates", 4 * DA, LANE),
            ("qkv_b", 4 * DA + LANE, 3 * DB))
HALO = 8


def _sequence_tile(x_ref, meta_ref, tile, n_tiles, n_pad):
    xb = x_ref[...]
    tm = xb.shape[0]
    first = jnp.concatenate([meta_ref[...], xb[:tm - N_META]], axis=0)
    if n_pad:
        last = jnp.concatenate([xb[n_pad:], jnp.zeros((n_pad, xb.shape[1]), xb.dtype)], axis=0)
    else:
        last = xb
    return jnp.where(tile == 0, first, jnp.where(tile == n_tiles - 1, last, xb))


def _hyb_in_kernel(x_ref, meta_ref, nw_ref, w_ref, convw_ref, gpar_ref,
                   h_ref, qkv_ref, zs_ref, gate_ref, qkvb_ref, tail_ref, *, n_pad):
    cols = {name: slice(start, start + width) for name, start, width in HYB_COLS}

    @pl.when(pl.program_id(1) == 0)
    def _():
        tail_ref[...] = jnp.zeros_like(tail_ref)

    h0 = _sequence_tile(x_ref, meta_ref, pl.program_id(1), pl.num_programs(1), n_pad)
    h_ref[0] = h0
    y = _rms(h0, nw_ref[...]).astype(BF16)
    first_rows = lax.broadcasted_iota(jnp.int32, (HALO, DA), 0)

    def conv_silu(pre, cs):
        tail = tail_ref[:, cs]
        acc = pre * convw_ref[CONV_WIDTH - 1:CONV_WIDTH, cs]
        for s in range(1, CONV_WIDTH):
            rolled = pltpu.roll(pre, s, 0)
            head = jnp.where(first_rows < s, pltpu.roll(tail, s, 0), rolled[:HALO])
            shifted = jnp.concatenate([head, rolled[HALO:]], axis=0)
            acc = acc + shifted * convw_ref[CONV_WIDTH - 1 - s:CONV_WIDTH - s, cs]
        tail_ref[:, cs] = pre[pre.shape[0] - HALO:, :]
        return acc * _sigmoid(acc)

    def unit_heads(xc, base, gain):
        for h in range(HA):
            t = xc[:, h * HEAD_DIM:(h + 1) * HEAD_DIM]
            t = t * (lax.rsqrt(jnp.sum(t * t, axis=-1, keepdims=True) + EPS) * gain)
            qkv_ref[0, :, base + h * HEAD_DIM:base + (h + 1) * HEAD_DIM] = t.astype(qkv_ref.dtype)

    def post_q(pre):
        unit_heads(conv_silu(pre, slice(0, DA)), 0, HEAD_DIM ** -0.5)

    def post_k(pre):
        unit_heads(conv_silu(pre, slice(DA, 2 * DA)), DA, 1.0)

    def post_v(pre):
        qkv_ref[0, :, 2 * DA:] = conv_silu(pre, slice(2 * DA, 3 * DA)).astype(qkv_ref.dtype)

    def post_z(zlin):
        zs_ref[0] = (zlin * _sigmoid(zlin)).astype(zs_ref.dtype)

    def post_gates(ba):
        g = -jnp.exp(gpar_ref[0:1, :]) * _softplus(ba + gpar_ref[1:2, :])
        gate_ref[0] = jnp.where(lax.broadcasted_iota(jnp.int32, ba.shape, 1) < HA, _sigmoid(ba), g)

    def post_b(part):
        def store(pre):
            qkvb_ref[0, :, part * DB:(part + 1) * DB] = pre.astype(qkvb_ref.dtype)
        return store

    qa, qb = cols["qkv_a"].start, cols["qkv_b"].start
    attn = [(slice(qb + part * DB, qb + (part + 1) * DB), post_b(part)) for part in range(3)]
    stages = [(slice(qa, qa + DA), post_q), attn[0], (slice(qa + DA, qa + 2 * DA), post_k), attn[1],
              (slice(qa + 2 * DA, qa + 3 * DA), post_v), attn[2], (cols["z_a"], post_z),
              (cols["gates"], post_gates)]
    pending = _dot(y, w_ref[:, stages[0][0]])
    for idx, (_, post) in enumerate(stages):
        upcoming = _dot(y, w_ref[:, stages[idx + 1][0]]) if idx + 1 < len(stages) else None
        post(pending)
        pending = upcoming


def _hyb_in(x, meta, lp, norm_w, w, conv_w, gate_par):
    b, seq, d = x.shape
    n_pad = lp - (N_META + seq)
    tm = next(rows for rows in (384, 256, 128) if lp % rows == 0)
    assert lp // tm >= 2 and seq >= tm and n_pad % 8 == 0 and n_pad <= tm
    row = lambda width: pl.BlockSpec((1, tm, width), lambda bi, i: (bi, i, 0))
    n_tiles = lp // tm

    def x_rows(bi, i):
        start = (i * tm - N_META + N_META * (i == 0).astype(jnp.int32)
                 - n_pad * (i == n_tiles - 1).astype(jnp.int32))
        return bi, pl.multiple_of(start, HALO), 0

    x_spec = pl.BlockSpec((None, pl.Element(tm), pl.Element(d)), x_rows)
    widths = {name: width for name, _, width in HYB_COLS}
    return pl.pallas_call(
        functools.partial(_hyb_in_kernel, n_pad=n_pad),
        grid=(b, lp // tm),
        in_specs=[x_spec, _const_spec((N_META, d)), _const_spec((1, d)), _const_spec(w.shape),
                  _const_spec(conv_w.shape), _const_spec(gate_par.shape)],
        out_specs=[row(d), row(widths["qkv_a"]), row(widths["z_a"]), row(widths["gates"]),
                   row(widths["qkv_b"])],
        out_shape=[jax.ShapeDtypeStruct((b, lp, d), F32),
                   jax.ShapeDtypeStruct((b, lp, widths["qkv_a"]), BF16),
                   jax.ShapeDtypeStruct((b, lp, widths["z_a"]), BF16),
                   jax.ShapeDtypeStruct((b, lp, widths["gates"]), F32),
                   jax.ShapeDtypeStruct((b, lp, widths["qkv_b"]), BF16)],
        scratch_shapes=[pltpu.VMEM((HALO, widths["qkv_a"]), F32)],
        compiler_params=_params("parallel", "arbitrary"),
        name="hyb_in",
    )(x, meta, norm_w.reshape(1, d), w, conv_w, gate_par)


def _run_stages(stage_generator):
    try:
        while True:
            next(stage_generator)
    except StopIteration as stop:
        return stop.value


def _unit_lower_inverses(lowers, ri, ci):
    c = lowers[0].shape[0]

    def same_block(size):
        shift = size.bit_length() - 1
        return (ri >> shift) == (ci >> shift)

    def mm(a, b):
        return _dot(a.astype(BF16), b.astype(BF16))

    eye = jnp.where(ci == ri, 1.0, 0.0).astype(F32)
    in_base = same_block(INV_BASE)
    xs = [jnp.where(in_base, -low, 0.0) for low in lowers]
    ps = [eye + x for x in xs]
    xs = [mm(x, x) for x in xs]
    yield
    covered = 2
    while 2 * covered < INV_BASE:
        both = [mm(jnp.concatenate([x, p], axis=0), x) for x, p in zip(xs, ps)]
        yield
        xs = [t[:c] for t in both]
        ps = [p + t[c:] for p, t in zip(ps, both)]
        covered *= 2
    ps = [p + mm(p, x) for p, x in zip(ps, xs)]
    yield
    size = INV_BASE
    while size < c:
        joins = jnp.logical_and(same_block(2 * size), jnp.logical_not(same_block(size)))
        ys = [mm(jnp.where(joins, low, 0.0), p) for low, p in zip(lowers, ps)]
        yield
        ps = [p - mm(p, y) for p, y in zip(ps, ys)]
        yield
        size *= 2
    return ps


def _deltanet_stages(qkv_ref, zs_ref, gate_ref, onorm_ref, o_ref, state_ref):
    nseq, c = qkv_ref.shape[0], qkv_ref.shape[1]
    units = [(s, h) for s in range(nseq) for h in range(HA)]
    n = range(len(units))

    @pl.when(pl.program_id(1) == 0)
    def _():
        state_ref[...] = jnp.zeros_like(state_ref)

    ri = lax.broadcasted_iota(jnp.int32, (c, c), 0)
    ci = lax.broadcasted_iota(jnp.int32, (c, c), 1)
    incl = ci <= ri
    strict = ci < ri
    tril = jnp.where(incl, 1.0, 0.0).astype(BF16)
    sel = jnp.where(lax.broadcasted_iota(jnp.int32, (8, LANE), 1)
                    == lax.broadcasted_iota(jnp.int32, (8, LANE), 0) + HA, 1.0, 0.0).astype(BF16)

    def exact_terms(t):
        hi = t.astype(BF16)
        rest = t - hi.astype(F32)
        mid = rest.astype(BF16)
        return hi, mid, (rest - mid.astype(F32)).astype(BF16)

    gates = [gate_ref[s] for s in range(nseq)]
    g_cum = [sum(_dot(tril, part) for part in exact_terms(g)) for g in gates]
    yield
    g_rows = [sum(_dot_nt(sel, part) for part in exact_terms(g)) for g in g_cum]
    yield

    def head_cols(base, s, h):
        return qkv_ref[s, :, base + h * HEAD_DIM:base + (h + 1) * HEAD_DIM]

    q_bf = [head_cols(0, s, h) for s, h in units]
    k_bf = [head_cols(DA, s, h) for s, h in units]
    q = [t.astype(F32) for t in q_bf]
    k = [t.astype(F32) for t in k_bf]
    v = [head_cols(2 * DA, s, h).astype(F32) for s, h in units]
    beta = [gates[s][:, h:h + 1] for s, h in units]
    g_col = [g_cum[s][:, HA + h:HA + h + 1] for s, h in units]
    g_row = [g_rows[s][h:h + 1, :] for s, h in units]
    g_last = [g_cum[s][c - 1:c, HA + h:HA + h + 1] for s, h in units]
    decay = [jnp.where(incl, jnp.exp(jnp.where(incl, g_col[u] - g_row[u], 0.0)), 0.0) for u in n]
    exp_g = [jnp.exp(g) for g in g_col]
    kb = [k[u] * beta[u] for u in n]
    yield
    scores = [_dot_nt(jnp.concatenate([kb[u].astype(BF16), q_bf[u]], axis=0), k_bf[u]) for u in n]
    yield
    lower = [jnp.where(strict, scores[u][:c] * decay[u], 0.0) for u in n]
    qk = [jnp.where(incl, scores[u][c:] * decay[u], 0.0).astype(BF16) for u in n]
    yield
    t_inv = yield from _unit_lower_inverses(lower, ri, ci)
    rhs = [jnp.concatenate([v[u] * beta[u], kb[u] * exp_g[u]], axis=1).astype(BF16) for u in n]
    sol = [_dot(t_inv[u].astype(BF16), rhs[u]) for u in n]
    yield
    q_dec = [(q[u] * exp_g[u]).astype(BF16) for u in n]
    k_dec = [(k[u] * jnp.exp(g_last[u] - g_col[u])).astype(BF16) for u in n]

    state = [state_ref[u] for u in n]
    state_bf = [s.astype(BF16) for s in state]
    from_state = [_dot(jnp.concatenate([sol[u][:, HEAD_DIM:].astype(BF16), q_dec[u]], axis=0), state_bf[u])
                  for u in n]
    yield
    v_new = [(sol[u][:, :HEAD_DIM] - from_state[u][:c]).astype(BF16) for u in n]
    o = [from_state[u][c:] + _dot(qk[u], v_new[u]) for u in n]
    yield
    for u in n:
        state_ref[u] = state[u] * jnp.exp(g_last[u]) + _dot_tn(k_dec[u], v_new[u])
    yield

    onorm = onorm_ref[...]
    for u, (s, h) in enumerate(units):
        sl = slice(h * HEAD_DIM, (h + 1) * HEAD_DIM)
        o_ref[s, :, sl] = (_rms(o[u], onorm) * zs_ref[s, :, sl].astype(F32)).astype(o_ref.dtype)


def _deltanet_kernel(*refs):
    _run_stages(_deltanet_stages(*refs))


def _deltanet(qkv, zs, gates, out_norm):
    b, lp, _ = qkv.shape
    c = DN_CHUNK
    nseq = DN_SEQS_PER_STEP
    return pl.pallas_call(
        _deltanet_kernel,
        grid=(b // nseq, lp // c),
        in_specs=[pl.BlockSpec((nseq, c, 3 * DA), lambda bi, i: (bi, i, 0)),
                  pl.BlockSpec((nseq, c, DA), lambda bi, i: (bi, i, 0)),
                  pl.BlockSpec((nseq, c, LANE), lambda bi, i: (bi, i, 0)),
                  _const_spec((1, HEAD_DIM))],
        out_specs=pl.BlockSpec((nseq, c, DA), lambda bi, i: (bi, i, 0)),
        out_shape=jax.ShapeDtypeStruct((b, lp, DA), BF16),
        scratch_shapes=[pltpu.VMEM((nseq * HA, HEAD_DIM, HEAD_DIM), F32)],
        compiler_params=_params("parallel", "arbitrary"),
        name="deltanet",
    )(qkv, zs, gates, out_norm)


def _stickbreak_stages(q_ref, k_ref, v_ref, heads, suffix, first, count, laters, diag_mask):
    tk = suffix.shape[0]
    n_heads = range(len(heads))
    chains = [(b, h) for b in range(count) for h in n_heads]
    index = [first - b for b in range(count)]
    starts = [pl.multiple_of(jnp.maximum(j, 0) * tk, tk) for j in index]

    def masked(c, t):
        b, _ = chains[c]
        if diag_mask is None:
            return t
        return jnp.where(diag_mask if b == 0 else index[b] >= 0, t, 0.0)

    n = range(len(chains))
    qs = [q_ref[seq, :, sl] for seq, sl in heads]
    zq = [_dot_nt(qs[h], k_ref[heads[h][0], pl.ds(starts[b], tk), heads[h][1]]) * (HEAD_DIM ** -0.5)
          for b, h in chains]
    yield
    log_beta = [jnp.minimum(z, 0.0) - jnp.log(1.0 + jnp.exp(-jnp.abs(z))) for z in zq]
    yield
    log_1m = [masked(c, log_beta[c] - zq[c]) for c in n]
    hi = [t.astype(BF16) for t in log_1m]
    lo = [(log_1m[c] - hi[c].astype(F32)).astype(BF16) for c in n]
    yield
    suf = [_dot(hi[c], suffix) + _dot(lo[c], suffix) for c in n]
    yield
    entering, running = {}, list(laters)
    for c, (b, h) in enumerate(chains):
        entering[c] = running[h]
        running[h] = running[h] + suf[c][:, 0:1]
    wgt = [masked(c, jnp.exp(log_beta[c] + (suf[c] - log_1m[c]) + entering[c])).astype(BF16) for c in n]
    yield
    pv = [_dot(wgt[c], v_ref[heads[h][0], pl.ds(starts[b], tk), heads[h][1]])
          for c, (b, h) in enumerate(chains)]
    yield
    contrib = tuple(functools.reduce(lambda x, y: x + y, [pv[c] for c in n if chains[c][1] == h])
                    for h in n_heads)
    largest = jnp.max(functools.reduce(jnp.maximum, running))
    return contrib, tuple(running), largest


def _stickbreak_kernel(q_ref, k_ref, v_ref, o_ref):
    tq = q_ref.shape[1]
    tk = tq
    i = pl.program_id(1)
    heads = [(seq, slice(h * HEAD_DIM, (h + 1) * HEAD_DIM))
             for seq in range(q_ref.shape[0]) for h in range(HB)]
    suffix = jnp.where(lax.broadcasted_iota(jnp.int32, (tk, tk), 0)
                       >= lax.broadcasted_iota(jnp.int32, (tk, tk), 1), 1.0, 0.0).astype(BF16)
    strictly_causal = (lax.broadcasted_iota(jnp.int32, (tq, tk), 1)
                       < lax.broadcasted_iota(jnp.int32, (tq, tk), 0))
    accs, laters, largest = _run_stages(_stickbreak_stages(
        q_ref, k_ref, v_ref, heads, suffix, i, ATT_FIRST_BLOCKS,
        (jnp.zeros((tq, 1), F32),) * len(heads), strictly_causal))

    def more(carry):
        j, largest, _, _ = carry
        return jnp.logical_and(j >= 0, largest > EXP_UNDERFLOW)

    def body(carry):
        j, _, accs, laters = carry
        contrib, laters, largest = _run_stages(_stickbreak_stages(
            q_ref, k_ref, v_ref, heads, suffix, j, 1, laters, None))
        return j - 1, largest, tuple(a + c for a, c in zip(accs, contrib)), laters

    _, _, accs, _ = lax.while_loop(more, body, (i - ATT_FIRST_BLOCKS, largest, accs, laters))
    for (seq, sl), acc in zip(heads, accs):
        o_ref[seq, :, sl] = acc.astype(o_ref.dtype)


def _stickbreak(qkv):
    b, lp, _ = qkv.shape
    tq = ATT_BLOCK
    nseq = ATT_SEQS_PER_STEP
    return pl.pallas_call(
        _stickbreak_kernel,
        grid=(b // nseq, lp // tq),
        in_specs=[pl.BlockSpec((nseq, tq, DB), lambda bi, i: (bi, i, 0)),
                  pl.BlockSpec((nseq, lp, DB), lambda bi, i: (bi, 0, 1)),
                  pl.BlockSpec((nseq, lp, DB), lambda bi, i: (bi, 0, 2))],
        out_specs=pl.BlockSpec((nseq, tq, DB), lambda bi, i: (bi, i, 0)),
        out_shape=jax.ShapeDtypeStruct((b, lp, DB), BF16),
        compiler_params=_params("parallel", "arbitrary"),
        name="stickbreak",
    )(qkv, qkv, qkv)


def _row_subtiles(rows):
    return [slice(r, r + FFN_SUBTILE) for r in range(0, rows, FFN_SUBTILE)]


def _residual_ffn(ms, subs, h_ref, npost_ref, nfpre_ref, wg_ref, wu_ref, wd_ref, nfpost_ref, out_ref):
    h1 = [h_ref[r, :] + _rms(m, npost_ref[...]) for m, r in zip(ms, subs)]
    hn = [_rms(t, nfpre_ref[...]).astype(BF16) for t in h1]
    g = [_dot(t, wg_ref[...]) for t in hn]
    u = [_dot(t, wu_ref[...]) for t in hn]
    a = [(gi * _sigmoid(gi) * ui).astype(BF16) for gi, ui in zip(g, u)]
    f = [_dot(t, wd_ref[...]) for t in a]
    for r, h1i, fi in zip(subs, h1, f):
        out_ref[r, :] = h1i + _rms(fi, nfpost_ref[...])


def _hyb_out_ffn_kernel(oa_ref, ob_ref, wo_ref, *rest):
    subs = _row_subtiles(oa_ref.shape[0])
    ms = [_dot(oa_ref[r, :], wo_ref[:DA, :]) + _dot(ob_ref[r, :], wo_ref[DA:, :]) for r in subs]
    _residual_ffn(ms, subs, *rest)


def _glu_out_ffn_kernel(y_ref, wglu_ref, *rest):
    d = wglu_ref.shape[1] // 2
    subs = _row_subtiles(y_ref.shape[0])
    yv = [_dot(y_ref[r, :], wglu_ref[...]) for r in subs]
    ms = [t[:, :d] * _sigmoid(t[:, d:]) for t in yv]
    _residual_ffn(ms, subs, *rest)


def _mixer_out_ffn(kernel_fn, acts, w_mix, h, npost, nfpre, wg, wu, wd, nfpost, keep=None):
    b, lp, d = h.shape
    dff = wg.shape[1]
    tm = 2 * FFN_SUBTILE
    weights = [_const_spec(w_mix.shape)]
    tail = [_const_spec((1, d)), _const_spec((1, d)), _const_spec((d, dff)), _const_spec((d, dff)),
            _const_spec((dff, d)), _const_spec((1, d))]
    if keep is None:
        rows = b * lp
        acts = [a.reshape(rows, a.shape[2]) for a in acts]
        h = h.reshape(rows, d)
        grid = (rows // tm,)
        row = lambda width: pl.BlockSpec((tm, width), lambda i: (i, 0))
        out_spec, out_shape, sem = row(d), (rows, d), ("parallel",)
    else:
        first, n_rows = keep
        assert first % N_META == 0 and tm % N_META == 0
        grid = (b, n_rows // tm)
        row = lambda width: pl.BlockSpec((None, pl.Element(tm), pl.Element(width)),
                                         lambda bi, i: (bi, pl.multiple_of(first + i * tm, N_META), 0))
        out_spec = pl.BlockSpec((None, tm, d), lambda bi, i: (bi, i, 0))
        out_shape, sem = (b, n_rows, d), ("parallel", "parallel")
    assert grid[-1] * tm == (b * lp if keep is None else keep[1])
    return pl.pallas_call(
        kernel_fn,
        grid=grid,
        in_specs=[row(a.shape[-1]) for a in acts] + weights + [row(d)] + tail,
        out_specs=out_spec,
        out_shape=jax.ShapeDtypeStruct(out_shape, F32),
        compiler_params=_params(*sem),
        name="mixer_out_ffn",
    )(*acts, w_mix, h, npost.reshape(1, d), nfpre.reshape(1, d), wg, wu, wd, nfpost.reshape(1, d))


def _s5_kernel(h_ref, nw_ref, win_ref, bmat_ref, cmat_ref, a_ref, d_ref, y_ref, state_ref, bu_ref,
               u_ref, swap_in_ref, swap_out_ref, hn_ref, hs_ref):
    bsz, tl, dm = h_ref.shape
    width = win_ref.shape[1]
    n_tiles = width // LANE
    pair = 2 * LANE

    @pl.when(pl.program_id(0) == 0)
    def _():
        state_ref[...] = jnp.zeros_like(state_ref)

    hn_ref[...] = _rms(h_ref[...].reshape(bsz * tl, dm), nw_ref[...]).astype(BF16)

    def input_map(k):
        slot = k % 2
        if k % 2 == 0:
            cols = slice(k * LANE, k * LANE + pair)
            hs_ref[:, :, cols] = _dot(hn_ref[...], win_ref[:, cols]).reshape(bsz, tl, pair)
        for b in range(bsz):
            swap_in_ref[slot, b * S5_PITCH:b * S5_PITCH + tl, :] = hs_ref[b, :, k * LANE:(k + 1) * LANE]
        u = jnp.concatenate([swap_in_ref[slot, pl.ds(t, bsz, stride=S5_PITCH), :] for t in range(tl)],
                            axis=0)
        u_ref[slot] = u
        bu_ref[slot] = _dot(u.astype(BF16), bmat_ref[k])

    input_map(0)
    for k in range(n_tiles):
        if k + 1 < n_tiles:
            input_map(k + 1)
        slot = k % 2
        buf = bu_ref.at[slot]
        a_re = jnp.broadcast_to(a_ref[k, 0:1, :], (bsz, S5_HALF))
        a_im = jnp.broadcast_to(a_ref[k, 1:2, :], (bsz, S5_HALF))
        st = state_ref[k]
        x_re, x_im = st[:, :S5_HALF], st[:, S5_HALF:]
        for t in range(tl):
            rs = slice(t * bsz, (t + 1) * bsz)
            bu = buf[rs, :]
            x_re, x_im = (a_re * x_re - a_im * x_im + bu[:, :S5_HALF],
                          a_re * x_im + a_im * x_re + bu[:, S5_HALF:])
            buf[rs, :] = jnp.concatenate([x_re, x_im], axis=1)
        state_ref[k] = jnp.concatenate([x_re, x_im], axis=1)

        lanes = slice(k * LANE, (k + 1) * LANE)
        y = jax.nn.gelu(_dot(buf[...].astype(BF16), cmat_ref[k]) + d_ref[:, lanes] * u_ref[slot])
        for t in range(tl):
            swap_out_ref[slot, pl.ds(t, bsz, stride=S5_PITCH), :] = y[t * bsz:(t + 1) * bsz, :]
        for b in range(bsz):
            y_ref[b, :, lanes] = swap_out_ref[slot, b * S5_PITCH:b * S5_PITCH + tl, :].astype(y_ref.dtype)


def _s5_mixer(h, norm_w, w_in, bmat, cmat, a_bar, d_skip):
    bsz, lp, dm = h.shape
    width = w_in.shape[1]
    tl = S5_TILE
    n_tiles = width // LANE
    assert n_tiles % 2 == 0
    return pl.pallas_call(
        _s5_kernel,
        grid=(lp // tl,),
        in_specs=[pl.BlockSpec((bsz, tl, dm), lambda i: (0, i, 0)),
                  _const_spec((1, dm)), _const_spec(w_in.shape),
                  _const_spec(bmat.shape), _const_spec(cmat.shape),
                  _const_spec(a_bar.shape), _const_spec((1, width))],
        out_specs=pl.BlockSpec((bsz, tl, width), lambda i: (0, i, 0)),
        out_shape=jax.ShapeDtypeStruct((bsz, lp, width), BF16),
        scratch_shapes=[pltpu.VMEM((n_tiles, bsz, 2 * S5_HALF), F32),
                        pltpu.VMEM((2, tl * bsz, 2 * S5_HALF), F32),
                        pltpu.VMEM((2, tl * bsz, LANE), F32),
                        pltpu.VMEM((2, bsz * S5_PITCH, LANE), F32),
                        pltpu.VMEM((2, bsz * S5_PITCH, LANE), F32),
                        pltpu.VMEM((tl * bsz, dm), BF16),
                        pltpu.VMEM((bsz, tl, width), F32)],
        compiler_params=_params("arbitrary"),
        name="s5_mixer",
    )(h, norm_w.reshape(1, dm), w_in, bmat, cmat, a_bar, d_skip.reshape(1, width))


def _s5_discretise_kernel(lre_ref, lim_ref, step_ref, bre_ref, bim_ref,
                          are_ref, aim_ref, bbre_ref, bbim_ref):
    lam_re, lam_im = lre_ref[...], lim_ref[...]
    dt = jnp.exp(step_ref[...])
    mag = jnp.exp(lam_re * dt)
    a_re = mag * jnp.cos(lam_im * dt)
    a_im = mag * jnp.sin(lam_im * dt)
    den = lam_re * lam_re + lam_im * lam_im
    f_re = ((a_re - 1.0) * lam_re + a_im * lam_im) / den
    f_im = (a_im * lam_re - (a_re - 1.0) * lam_im) / den
    are_ref[...] = a_re
    aim_ref[...] = a_im
    bbre_ref[...] = f_re * bre_ref[...] - f_im * bim_ref[...]
    bbim_ref[...] = f_re * bim_ref[...] + f_im * bre_ref[...]


def _s5_discretise(lam_re, lam_im, log_step, b_re, b_im, c_re, c_im):
    n_groups = lam_re.shape[0]
    n_tiles = n_groups // GROUPS_PER_LANE_TILE
    flat = (n_groups, STATE * GROUP)
    spread = lambda t: jnp.repeat(t, GROUP, axis=1)
    a_re, a_im, bb_re, bb_im = pl.pallas_call(
        _s5_discretise_kernel,
        out_shape=[jax.ShapeDtypeStruct(flat, F32)] * 4,
        name="s5_discretise",
    )(spread(lam_re), spread(lam_im), log_step.reshape(n_groups, 1),
      b_re.reshape(flat), b_im.reshape(flat))
    a_re, a_im = a_re[:, ::GROUP], a_im[:, ::GROUP]
    bb_re = bb_re.reshape(n_groups, STATE, GROUP)
    bb_im = bb_im.reshape(n_groups, STATE, GROUP)
    eye = jnp.eye(GROUPS_PER_LANE_TILE, dtype=F32)

    def block_diagonal(t):
        t = t[:, :, :, None, :] * eye[None, :, None, :, None]
        return t.reshape(n_tiles, t.shape[1] * t.shape[2], t.shape[3] * t.shape[4])

    def in_map(bb):
        t = bb.reshape(n_tiles, GROUPS_PER_LANE_TILE, STATE, GROUP)
        return block_diagonal(jnp.swapaxes(t, 2, 3))

    def out_map(cc):
        t = cc.reshape(n_tiles, GROUPS_PER_LANE_TILE, GROUP, STATE)
        return block_diagonal(jnp.swapaxes(t, 2, 3))

    bmat = jnp.concatenate([in_map(bb_re), in_map(bb_im)], axis=2).astype(BF16)
    cmat = jnp.concatenate([out_map(c_re), -out_map(c_im)], axis=1).astype(BF16)
    a_bar = jnp.stack([a_re.reshape(n_tiles, S5_HALF), a_im.reshape(n_tiles, S5_HALF)], axis=1)
    return bmat, cmat, a_bar


def kernel(x, meta_tokens, norm_mix_pre, norm_mix_post, norm_ffn_pre, norm_ffn_post, ffn_w_gate,
           ffn_w_up, ffn_w_down, hyb_w_in, hyb_conv_w, dn_a_log, dn_dt_bias, dn_out_norm, hyb_w_out,
           ssm_w_in, ssm_lambda_re, ssm_lambda_im, ssm_log_step, ssm_b_re, ssm_b_im, ssm_c_re,
           ssm_c_im, ssm_d, ssm_w_glu):
    bsz, seq, d = x.shape
    l_real = N_META + seq
    lp = -(-l_real // SEQ_ALIGN) * SEQ_ALIGN
    w_in = hyb_w_in[0]
    n_gate = 2 * HA
    w_cat = jnp.concatenate([
        w_in[:, :4 * DA],
        jnp.pad(w_in[:, 4 * DA:4 * DA + n_gate], ((0, 0), (0, LANE - n_gate))),
        w_in[:, 4 * DA + n_gate:]], axis=1).astype(BF16)
    gate_par = jnp.zeros((2, LANE), F32)
    gate_par = gate_par.at[0, HA:2 * HA].set(dn_a_log[0].astype(F32))
    gate_par = gate_par.at[1, HA:2 * HA].set(dn_dt_bias[0].astype(F32))
    h, qkv_a, zs_a, gates, qkv_b = _hyb_in(x, meta_tokens.astype(x.dtype), lp, norm_mix_pre[0], w_cat,
                                           hyb_conv_w[0].astype(F32), gate_par)
    o_a = _deltanet(qkv_a, zs_a, gates, dn_out_norm[0].reshape(1, HEAD_DIM).astype(F32))
    o_b = _stickbreak(qkv_b)
    h = _mixer_out_ffn(_hyb_out_ffn_kernel, [o_a, o_b], hyb_w_out[0].astype(BF16), h,
                       norm_mix_post[0], norm_ffn_pre[0], ffn_w_gate[0].astype(BF16),
                       ffn_w_up[0].astype(BF16), ffn_w_down[0].astype(BF16), norm_ffn_post[0])

    h = h.reshape(bsz, lp, d)
    bmat, cmat, a_bar = _s5_discretise(ssm_lambda_re[0].astype(F32), ssm_lambda_im[0].astype(F32),
                                       ssm_log_step[0].astype(F32), ssm_b_re[0].astype(F32),
                                       ssm_b_im[0].astype(F32), ssm_c_re[0].astype(F32),
                                       ssm_c_im[0].astype(F32))
    y = _s5_mixer(h, norm_mix_pre[1], ssm_w_in[0].astype(BF16), bmat, cmat, a_bar,
                  ssm_d[0].astype(F32))
    return _mixer_out_ffn(_glu_out_ffn_kernel, [y], ssm_w_glu[0].astype(BF16),
                          h, norm_mix_post[1], norm_ffn_pre[1],
                          ffn_w_gate[1].astype(BF16), ffn_w_up[1].astype(BF16),
                          ffn_w_down[1].astype(BF16), norm_ffn_post[1], keep=(N_META, seq))
```

```python
import functools

import jax
import jax.numpy as jnp
from jax import lax
from jax.experimental import pallas as pl
from jax.experimental.pallas import tpu as pltpu

N_META = 16
HEAD_DIM = 128
HA = 4
HB = 4
DA = HA * HEAD_DIM
DB = HB * HEAD_DIM
CONV_WIDTH = 4
GROUP = 16
STATE = 64
EPS = 1e-6

LANE = 128
SEQ_ALIGN = 128
DN_CHUNK = 128
DN_SEQS_PER_STEP = 4
INV_BASE = 16
ATT_BLOCK = 128
ATT_SEQS_PER_STEP = 2
ATT_FIRST_BLOCKS = 3
FFN_SUBTILE = 256
S5_TILE = 64
S5_PITCH = S5_TILE + 8
GROUPS_PER_LANE_TILE = LANE // GROUP
S5_HALF = GROUPS_PER_LANE_TILE * STATE
VMEM_LIMIT = 56 * 1024 * 1024
EXP_UNDERFLOW = -105.0

F32 = jnp.float32
BF16 = jnp.bfloat16


def _dot(a, b):
    return jnp.dot(a, b, preferred_element_type=F32)


def _dot_nt(a, b):
    return lax.dot_general(a, b, (((1,), (1,)), ((), ())), preferred_element_type=F32)


def _dot_tn(a, b):
    return lax.dot_general(a, b, (((0,), (0,)), ((), ())), preferred_element_type=F32)


def _rms(x, w):
    return x * lax.rsqrt(jnp.mean(x * x, axis=-1, keepdims=True) + EPS) * w


def _sigmoid(x):
    return 1.0 / (1.0 + jnp.exp(-x))


def _softplus(x):
    return jnp.maximum(x, 0.0) + jnp.log(1.0 + jnp.exp(-jnp.abs(x)))


def _const_spec(shape):
    zeros = (0,) * len(shape)
    return pl.BlockSpec(shape, lambda *_: zeros, pipeline_mode=pl.Buffered(1))


def _params(*sem):
    return pltpu.CompilerParams(dimension_semantics=sem, vmem_limit_bytes=VMEM_LIMIT)


HYB_COLS = (("qkv_a", 0, 3 * DA), ("z_a", 3 * DA, DA), ("gates", 4 * DA, LANE),
            ("qkv_b", 4 * DA + LANE, 3 * DB))
HALO = 8


def _sequence_tile(x_ref, meta_ref, tile, n_tiles, n_pad):
    xb = x_ref[...]
    tm = xb.shape[0]
    first = jnp.concatenate([meta_ref[...], xb[:tm - N_META]], axis=0)
    if n_pad:
        last = jnp.concatenate([xb[n_pad:], jnp.zeros((n_pad, xb.shape[1]), xb.dtype)], axis=0)
    else:
        last = xb
    return jnp.where(tile == 0, first, jnp.where(tile == n_tiles - 1, last, xb))


def _hyb_in_kernel(x_ref, meta_ref, nw_ref, w_ref, convw_ref, gpar_ref,
                   h_ref, qkv_ref, zs_ref, gate_ref, qkvb_ref, tail_ref, *, n_pad):
    cols = {name: slice(start, start + width) for name, start, width in HYB_COLS}

    @pl.when(pl.program_id(1) == 0)
    def _():
        tail_ref[...] = jnp.zeros_like(tail_ref)

    h0 = _sequence_tile(x_ref, meta_ref, pl.program_id(1), pl.num_programs(1), n_pad)
    h_ref[0] = h0
    y = _rms(h0, nw_ref[...]).astype(BF16)
    first_rows = lax.broadcasted_iota(jnp.int32, (HALO, DA), 0)

    def conv_silu(pre, cs):
        tail = tail_ref[:, cs]
        acc = pre * convw_ref[CONV_WIDTH - 1:CONV_WIDTH, cs]
        for s in range(1, CONV_WIDTH):
            rolled = pltpu.roll(pre, s, 0)
            head = jnp.where(first_rows < s, pltpu.roll(tail, s, 0), rolled[:HALO])
            shifted = jnp.concatenate([head, rolled[HALO:]], axis=0)
            acc = acc + shifted * convw_ref[CONV_WIDTH - 1 - s:CONV_WIDTH - s, cs]
        tail_ref[:, cs] = pre[pre.shape[0] - HALO:, :]
        return acc * _sigmoid(acc)

    def unit_heads(xc, base, gain):
        for h in range(HA):
            t = xc[:, h * HEAD_DIM:(h + 1) * HEAD_DIM]
            t = t * (lax.rsqrt(jnp.sum(t * t, axis=-1, keepdims=True) + EPS) * gain)
            qkv_ref[0, :, base + h * HEAD_DIM:base + (h + 1) * HEAD_DIM] = t.astype(qkv_ref.dtype)

    def post_q(pre):
        unit_heads(conv_silu(pre, slice(0, DA)), 0, HEAD_DIM ** -0.5)

    def post_k(pre):
        unit_heads(conv_silu(pre, slice(DA, 2 * DA)), DA, 1.0)

    def post_v(pre):
        qkv_ref[0, :, 2 * DA:] = conv_silu(pre, slice(2 * DA, 3 * DA)).astype(qkv_ref.dtype)

    def post_z(zlin):
        zs_ref[0] = (zlin * _sigmoid(zlin)).astype(zs_ref.dtype)

    def post_gates(ba):
        g = -jnp.exp(gpar_ref[0:1, :]) * _softplus(ba + gpar_ref[1:2, :])
        gate_ref[0] = jnp.where(lax.broadcasted_iota(jnp.int32, ba.shape, 1) < HA, _sigmoid(ba), g)

    def post_b(part):
        def store(pre):
            qkvb_ref[0, :, part * DB:(part + 1) * DB] = pre.astype(qkvb_ref.dtype)
        return store

    qa, qb = cols["qkv_a"].start, cols["qkv_b"].start
    attn = [(slice(qb + part * DB, qb + (part + 1) * DB), post_b(part)) for part in range(3)]
    stages = [(slice(qa, qa + DA), post_q), attn[0], (slice(qa + DA, qa + 2 * DA), post_k), attn[1],
              (slice(qa + 2 * DA, qa + 3 * DA), post_v), attn[2], (cols["z_a"], post_z),
              (cols["gates"], post_gates)]
    pending = _dot(y, w_ref[:, stages[0][0]])
    for idx, (_, post) in enumerate(stages):
        upcoming = _dot(y, w_ref[:, stages[idx + 1][0]]) if idx + 1 < len(stages) else None
        post(pending)
        pending = upcoming


def _hyb_in(x, meta, lp, norm_w, w, conv_w, gate_par):
    b, seq, d = x.shape
    n_pad = lp - (N_META + seq)
    tm = next(rows for rows in (384, 256, 128) if lp % rows == 0)
    assert lp // tm >= 2 and seq >= tm and n_pad % 8 == 0 and n_pad <= tm
    row = lambda width: pl.BlockSpec((1, tm, width), lambda bi, i: (bi, i, 0))
    n_tiles = lp // tm

    def x_rows(bi, i):
        start = (i * tm - N_META + N_META * (i == 0).astype(jnp.int32)
                 - n_pad * (i == n_tiles - 1).astype(jnp.int32))
        return bi, pl.multiple_of(start, HALO), 0

    x_spec = pl.BlockSpec((None, pl.Element(tm), pl.Element(d)), x_rows)
    widths = {name: width for name, _, width in HYB_COLS}
    return pl.pallas_call(
        functools.partial(_hyb_in_kernel, n_pad=n_pad),
        grid=(b, lp // tm),
        in_specs=[x_spec, _const_spec((N_META, d)), _const_spec((1, d)), _const_spec(w.shape),
                  _const_spec(conv_w.shape), _const_spec(gate_par.shape)],
        out_specs=[row(d), row(widths["qkv_a"]), row(widths["z_a"]), row(widths["gates"]),
                   row(widths["qkv_b"])],
        out_shape=[jax.ShapeDtypeStruct((b, lp, d), F32),
                   jax.ShapeDtypeStruct((b, lp, widths["qkv_a"]), BF16),
                   jax.ShapeDtypeStruct((b, lp, widths["z_a"]), BF16),
                   jax.ShapeDtypeStruct((b, lp, widths["gates"]), F32),
                   jax.ShapeDtypeStruct((b, lp, widths["qkv_b"]), BF16)],
        scratch_shapes=[pltpu.VMEM((HALO, widths["qkv_a"]), F32)],
        compiler_params=_params("parallel", "arbitrary"),
        name="hyb_in",
    )(x, meta, norm_w.reshape(1, d), w, conv_w, gate_par)


def _run_stages(stage_generator):
    try:
        while True:
            next(stage_generator)
    except StopIteration as stop:
        return stop.value


def _unit_lower_inverses(lowers, ri, ci):
    c = lowers[0].shape[0]

    def same_block(size):
        shift = size.bit_length() - 1
        return (ri >> shift) == (ci >> shift)

    def mm(a, b):
        return _dot(a.astype(BF16), b.astype(BF16))

    eye = jnp.where(ci == ri, 1.0, 0.0).astype(F32)
    in_base = same_block(INV_BASE)
    xs = [jnp.where(in_base, -low, 0.0) for low in lowers]
    ps = [eye + x for x in xs]
    xs = [mm(x, x) for x in xs]
    yield
    covered = 2
    while 2 * covered < INV_BASE:
        both = [mm(jnp.concatenate([x, p], axis=0), x) for x, p in zip(xs, ps)]
        yield
        xs = [t[:c] for t in both]
        ps = [p + t[c:] for p, t in zip(ps, both)]
        covered *= 2
    ps = [p + mm(p, x) for p, x in zip(ps, xs)]
    yield
    size = INV_BASE
    while size < c:
        joins = jnp.logical_and(same_block(2 * size), jnp.logical_not(same_block(size)))
        ys = [mm(jnp.where(joins, low, 0.0), p) for low, p in zip(lowers, ps)]
        yield
        ps = [p - mm(p, y) for p, y in zip(ps, ys)]
        yield
        size *= 2
    return ps


def _deltanet_stages(qkv_ref, zs_ref, gate_ref, onorm_ref, o_ref, state_ref):
    nseq, c = qkv_ref.shape[0], qkv_ref.shape[1]
    units = [(s, h) for s in range(nseq) for h in range(HA)]
    n = range(len(units))

    @pl.when(pl.program_id(1) == 0)
    def _():
        state_ref[...] = jnp.zeros_like(state_ref)

    ri = lax.broadcasted_iota(jnp.int32, (c, c), 0)
    ci = lax.broadcasted_iota(jnp.int32, (c, c), 1)
    incl = ci <= ri
    strict = ci < ri
    tril = jnp.where(incl, 1.0, 0.0).astype(BF16)
    sel = jnp.where(lax.broadcasted_iota(jnp.int32, (8, LANE), 1)
                    == lax.broadcasted_iota(jnp.int32, (8, LANE), 0) + HA, 1.0, 0.0).astype(BF16)

    def exact_terms(t):
        hi = t.astype(BF16)
        rest = t - hi.astype(F32)
        mid = rest.astype(BF16)
        return hi, mid, (rest - mid.astype(F32)).astype(BF16)

    gates = [gate_ref[s] for s in range(nseq)]
    g_cum = [sum(_dot(tril, part) for part in exact_terms(g)) for g in gates]
    yield
    g_rows = [sum(_dot_nt(sel, part) for part in exact_terms(g)) for g in g_cum]
    yield

    def head_cols(base, s, h):
        return qkv_ref[s, :, base + h * HEAD_DIM:base + (h + 1) * HEAD_DIM]

    q_bf = [head_cols(0, s, h) for s, h in units]
    k_bf = [head_cols(DA, s, h) for s, h in units]
    q = [t.astype(F32) for t in q_bf]
    k = [t.astype(F32) for t in k_bf]
    v = [head_cols(2 * DA, s, h).astype(F32) for s, h in units]
    beta = [gates[s][:, h:h + 1] for s, h in units]
    g_col = [g_cum[s][:, HA + h:HA + h + 1] for s, h in units]
    g_row = [g_rows[s][h:h + 1, :] for s, h in units]
    g_last = [g_cum[s][c - 1:c, HA + h:HA + h + 1] for s, h in units]
    decay = [jnp.where(incl, jnp.exp(jnp.where(incl, g_col[u] - g_row[u], 0.0)), 0.0) for u in n]
    exp_g = [jnp.exp(g) for g in g_col]
    kb = [k[u] * beta[u] for u in n]
    yield
    scores = [_dot_nt(jnp.concatenate([kb[u].astype(BF16), q_bf[u]], axis=0), k_bf[u]) for u in n]
    yield
    lower = [jnp.where(strict, scores[u][:c] * decay[u], 0.0) for u in n]
    qk = [jnp.where(incl, scores[u][c:] * decay[u], 0.0).astype(BF16) for u in n]
    yield
    t_inv = yield from _unit_lower_inverses(lower, ri, ci)
    rhs = [jnp.concatenate([v[u] * beta[u], kb[u] * exp_g[u]], axis=1).astype(BF16) for u in n]
    sol = [_dot(t_inv[u].astype(BF16), rhs[u]) for u in n]
    yield
    q_dec = [(q[u] * exp_g[u]).astype(BF16) for u in n]
    k_dec = [(k[u] * jnp.exp(g_last[u] - g_col[u])).astype(BF16) for u in n]

    state = [state_ref[u] for u in n]
    state_bf = [s.astype(BF16) for s in state]
    from_state = [_dot(jnp.concatenate([sol[u][:, HEAD_DIM:].astype(BF16), q_dec[u]], axis=0), state_bf[u])
                  for u in n]
    yield
    v_new = [(sol[u][:, :HEAD_DIM] - from_state[u][:c]).astype(BF16) for u in n]
    o = [from_state[u][c:] + _dot(qk[u], v_new[u]) for u in n]
    yield
    for u in n:
        state_ref[u] = state[u] * jnp.exp(g_last[u]) + _dot_tn(k_dec[u], v_new[u])
    yield

    onorm = onorm_ref[...]
    for u, (s, h) in enumerate(units):
        sl = slice(h * HEAD_DIM, (h + 1) * HEAD_DIM)
        o_ref[s, :, sl] = (_rms(o[u], onorm) * zs_ref[s, :, sl].astype(F32)).astype(o_ref.dtype)


def _deltanet_kernel(*refs):
    _run_stages(_deltanet_stages(*refs))


def _deltanet(qkv, zs, gates, out_norm):
    b, lp, _ = qkv.shape
    c = DN_CHUNK
    nseq = DN_SEQS_PER_STEP
    return pl.pallas_call(
        _deltanet_kernel,
        grid=(b // nseq, lp // c),
        in_specs=[pl.BlockSpec((nseq, c, 3 * DA), lambda bi, i: (bi, i, 0)),
                  pl.BlockSpec((nseq, c, DA), lambda bi, i: (bi, i, 0)),
                  pl.BlockSpec((nseq, c, LANE), lambda bi, i: (bi, i, 0)),
                  _const_spec((1, HEAD_DIM))],
        out_specs=pl.BlockSpec((nseq, c, DA), lambda bi, i: (bi, i, 0)),
        out_shape=jax.ShapeDtypeStruct((b, lp, DA), BF16),
        scratch_shapes=[pltpu.VMEM((nseq * HA, HEAD_DIM, HEAD_DIM), F32)],
        compiler_params=_params("parallel", "arbitrary"),
        name="deltanet",
    )(qkv, zs, gates, out_norm)


def _stickbreak_stages(q_ref, k_ref, v_ref, heads, suffix, first, count, laters, diag_mask):
    tk = suffix.shape[0]
    n_heads = range(len(heads))
    chains = [(b, h) for b in range(count) for h in n_heads]
    index = [first - b for b in range(count)]
    starts = [pl.multiple_of(jnp.maximum(j, 0) * tk, tk) for j in index]

    def masked(c, t):
        b, _ = chains[c]
        if diag_mask is None:
            return t
        return jnp.where(diag_mask if b == 0 else index[b] >= 0, t, 0.0)

    n = range(len(chains))
    qs = [q_ref[seq, :, sl] for seq, sl in heads]
    zq = [_dot_nt(qs[h], k_ref[heads[h][0], pl.ds(starts[b], tk), heads[h][1]]) * (HEAD_DIM ** -0.5)
          for b, h in chains]
    yield
    log_1m = [masked(c, jnp.minimum(-zq[c], 0.0) - jnp.log(1.0 + jnp.exp(-jnp.abs(zq[c])))) for c in n]
    yield
    hi = [t.astype(BF16) for t in log_1m]
    lo = [(log_1m[c] - hi[c].astype(F32)).astype(BF16) for c in n]
    yield
    suf = [_dot(hi[c], suffix) + _dot(lo[c], suffix) for c in n]
    yield
    entering, running = {}, list(laters)
    for c, (b, h) in enumerate(chains):
        entering[c] = running[h]
        running[h] = running[h] + suf[c][:, 0:1]
    wgt = [masked(c, jnp.exp(zq[c] + suf[c] + entering[c])).astype(BF16) for c in n]
    yield
    pv = [_dot(wgt[c], v_ref[heads[h][0], pl.ds(starts[b], tk), heads[h][1]])
          for c, (b, h) in enumerate(chains)]
    yield
    contrib = tuple(functools.reduce(lambda x, y: x + y, [pv[c] for c in n if chains[c][1] == h])
                    for h in n_heads)
    largest = jnp.max(functools.reduce(jnp.maximum, running))
    return contrib, tuple(running), largest


def _stickbreak_kernel(q_ref, k_ref, v_ref, o_ref):
    tq = q_ref.shape[1]
    tk = tq
    i = pl.program_id(1)
    heads = [(seq, slice(h * HEAD_DIM, (h + 1) * HEAD_DIM))
             for seq in range(q_ref.shape[0]) for h in range(HB)]
    suffix = jnp.where(lax.broadcasted_iota(jnp.int32, (tk, tk), 0)
                       >= lax.broadcasted_iota(jnp.int32, (tk, tk), 1), 1.0, 0.0).astype(BF16)
    strictly_causal = (lax.broadcasted_iota(jnp.int32, (tq, tk), 1)
                       < lax.broadcasted_iota(jnp.int32, (tq, tk), 0))
    accs, laters, largest = _run_stages(_stickbreak_stages(
        q_ref, k_ref, v_ref, heads, suffix, i, ATT_FIRST_BLOCKS,
        (jnp.zeros((tq, 1), F32),) * len(heads), strictly_causal))

    def more(carry):
        j, largest, _, _ = carry
        return jnp.logical_and(j >= 0, largest > EXP_UNDERFLOW)

    def body(carry):
        j, _, accs, laters = carry
        contrib, laters, largest = _run_stages(_stickbreak_stages(
            q_ref, k_ref, v_ref, heads, suffix, j, 1, laters, None))
        return j - 1, largest, tuple(a + c for a, c in zip(accs, contrib)), laters

    _, _, accs, _ = lax.while_loop(more, body, (i - ATT_FIRST_BLOCKS, largest, accs, laters))
    for (seq, sl), acc in zip(heads, accs):
        o_ref[seq, :, sl] = acc.astype(o_ref.dtype)


def _stickbreak(qkv):
    b, lp, _ = qkv.shape
    tq = ATT_BLOCK
    nseq = ATT_SEQS_PER_STEP
    return pl.pallas_call(
        _stickbreak_kernel,
        grid=(b // nseq, lp // tq),
        in_specs=[pl.BlockSpec((nseq, tq, DB), lambda bi, i: (bi, i, 0)),
                  pl.BlockSpec((nseq, lp, DB), lambda bi, i: (bi, 0, 1)),
                  pl.BlockSpec((nseq, lp, DB), lambda bi, i: (bi, 0, 2))],
        out_specs=pl.BlockSpec((nseq, tq, DB), lambda bi, i: (bi, i, 0)),
        out_shape=jax.ShapeDtypeStruct((b, lp, DB), BF16),
        compiler_params=_params("parallel", "arbitrary"),
        name="stickbreak",
    )(qkv, qkv, qkv)


def _row_subtiles(rows):
    return [slice(r, r + FFN_SUBTILE) for r in range(0, rows, FFN_SUBTILE)]


def _residual_ffn(ms, subs, h_ref, npost_ref, nfpre_ref, wg_ref, wu_ref, wd_ref, nfpost_ref, out_ref):
    h1 = [h_ref[r, :] + _rms(m, npost_ref[...]) for m, r in zip(ms, subs)]
    hn = [_rms(t, nfpre_ref[...]).astype(BF16) for t in h1]
    g = [_dot(t, wg_ref[...]) for t in hn]
    u = [_dot(t, wu_ref[...]) for t in hn]
    a = [(gi * _sigmoid(gi) * ui).astype(BF16) for gi, ui in zip(g, u)]
    f = [_dot(t, wd_ref[...]) for t in a]
    for r, h1i, fi in zip(subs, h1, f):
        out_ref[r, :] = h1i + _rms(fi, nfpost_ref[...])


def _hyb_out_ffn_kernel(oa_ref, ob_ref, wo_ref, *rest):
    subs = _row_subtiles(oa_ref.shape[0])
    ms = [_dot(oa_ref[r, :], wo_ref[:DA, :]) + _dot(ob_ref[r, :], wo_ref[DA:, :]) for r in subs]
    _residual_ffn(ms, subs, *rest)


def _glu_out_ffn_kernel(y_ref, wglu_ref, *rest):
    d = wglu_ref.shape[1] // 2
    subs = _row_subtiles(y_ref.shape[0])
    yv = [_dot(y_ref[r, :], wglu_ref[...]) for r in subs]
    ms = [t[:, :d] * _sigmoid(t[:, d:]) for t in yv]
    _residual_ffn(ms, subs, *rest)


def _mixer_out_ffn(kernel_fn, acts, w_mix, h, npost, nfpre, wg, wu, wd, nfpost, keep=None):
    b, lp, d = h.shape
    dff = wg.shape[1]
    n_rows = b * lp if keep is None else keep[1]
    tm = next(n for n in (3, 2) if n_rows % (n * FFN_SUBTILE) == 0) * FFN_SUBTILE
    weights = [_const_spec(w_mix.shape)]
    tail = [_const_spec((1, d)), _const_spec((1, d)), _const_spec((d, dff)), _const_spec((d, dff)),
            _const_spec((dff, d)), _const_spec((1, d))]
    if keep is None:
        rows = b * lp
        acts = [a.reshape(rows, a.shape[2]) for a in acts]
        h = h.reshape(rows, d)
        grid = (rows // tm,)
        row = lambda width: pl.BlockSpec((tm, width), lambda i: (i, 0))
        out_spec, out_shape, sem = row(d), (rows, d), ("parallel",)
    else:
        first, n_rows = keep
        assert first % N_META == 0 and tm % N_META == 0
        grid = (b, n_rows // tm)
        row = lambda width: pl.BlockSpec((None, pl.Element(tm), pl.Element(width)),
                                         lambda bi, i: (bi, pl.multiple_of(first + i * tm, N_META), 0))
        out_spec = pl.BlockSpec((None, tm, d), lambda bi, i: (bi, i, 0))
        out_shape, sem = (b, n_rows, d), ("parallel", "parallel")
    assert grid[-1] * tm == (b * lp if keep is None else keep[1])
    return pl.pallas_call(
        kernel_fn,
        grid=grid,
        in_specs=[row(a.shape[-1]) for a in acts] + weights + [row(d)] + tail,
        out_specs=out_spec,
        out_shape=jax.ShapeDtypeStruct(out_shape, F32),
        compiler_params=_params(*sem),
        name="mixer_out_ffn",
    )(*acts, w_mix, h, npost.reshape(1, d), nfpre.reshape(1, d), wg, wu, wd, nfpost.reshape(1, d))


def _s5_kernel(h_ref, nw_ref, win_ref, bmat_ref, cmat_ref, a_ref, d_ref, y_ref, state_ref, bu_ref,
               u_ref, swap_in_ref, swap_out_ref, hn_ref, hs_ref):
    bsz, tl, dm = h_ref.shape
    width = win_ref.shape[1]
    n_tiles = width // LANE
    pair = 2 * LANE

    @pl.when(pl.program_id(0) == 0)
    def _():
        state_ref[...] = jnp.zeros_like(state_ref)

    hn_ref[...] = _rms(h_ref[...].reshape(bsz * tl, dm), nw_ref[...]).astype(BF16)

    def input_map(k):
        slot = k % 2
        if k % 2 == 0:
            cols = slice(k * LANE, k * LANE + pair)
            hs_ref[:, :, cols] = _dot(hn_ref[...], win_ref[:, cols]).reshape(bsz, tl, pair)
        for b in range(bsz):
            swap_in_ref[slot, b * S5_PITCH:b * S5_PITCH + tl, :] = hs_ref[b, :, k * LANE:(k + 1) * LANE]
        u = jnp.concatenate([swap_in_ref[slot, pl.ds(t, bsz, stride=S5_PITCH), :] for t in range(tl)],
                            axis=0)
        u_ref[slot] = u
        bu_ref[slot] = _dot(u.astype(BF16), bmat_ref[k])

    input_map(0)
    for k in range(n_tiles):
        if k + 1 < n_tiles:
            input_map(k + 1)
        slot = k % 2
        buf = bu_ref.at[slot]
        a_re = jnp.broadcast_to(a_ref[k, 0:1, :], (bsz, S5_HALF))
        a_im = jnp.broadcast_to(a_ref[k, 1:2, :], (bsz, S5_HALF))
        st = state_ref[k]
        x_re, x_im = st[:, :S5_HALF], st[:, S5_HALF:]
        for t in range(tl):
            rs = slice(t * bsz, (t + 1) * bsz)
            bu = buf[rs, :]
            x_re, x_im = (a_re * x_re - a_im * x_im + bu[:, :S5_HALF],
                          a_re * x_im + a_im * x_re + bu[:, S5_HALF:])
            buf[rs, :] = jnp.concatenate([x_re, x_im], axis=1)
        state_ref[k] = jnp.concatenate([x_re, x_im], axis=1)

        lanes = slice(k * LANE, (k + 1) * LANE)
        y = jax.nn.gelu(_dot(buf[...].astype(BF16), cmat_ref[k]) + d_ref[:, lanes] * u_ref[slot])
        for t in range(tl):
            swap_out_ref[slot, pl.ds(t, bsz, stride=S5_PITCH), :] = y[t * bsz:(t + 1) * bsz, :]
        for b in range(bsz):
            y_ref[b, :, lanes] = swap_out_ref[slot, b * S5_PITCH:b * S5_PITCH + tl, :].astype(y_ref.dtype)


def _s5_mixer(h, norm_w, w_in, bmat, cmat, a_bar, d_skip):
    bsz, lp, dm = h.shape
    width = w_in.shape[1]
    tl = S5_TILE
    n_tiles = width // LANE
    assert n_tiles % 2 == 0
    return pl.pallas_call(
        _s5_kernel,
        grid=(lp // tl,),
        in_specs=[pl.BlockSpec((bsz, tl, dm), lambda i: (0, i, 0)),
                  _const_spec((1, dm)), _const_spec(w_in.shape),
                  _const_spec(bmat.shape), _const_spec(cmat.shape),
                  _const_spec(a_bar.shape), _const_spec((1, width))],
        out_specs=pl.BlockSpec((bsz, tl, width), lambda i: (0, i, 0)),
        out_shape=jax.ShapeDtypeStruct((bsz, lp, width), BF16),
        scratch_shapes=[pltpu.VMEM((n_tiles, bsz, 2 * S5_HALF), F32),
                        pltpu.VMEM((2, tl * bsz, 2 * S5_HALF), F32),
                        pltpu.VMEM((2, tl * bsz, LANE), F32),
                        pltpu.VMEM((2, bsz * S5_PITCH, LANE), F32),
                        pltpu.VMEM((2, bsz * S5_PITCH, LANE), F32),
                        pltpu.VMEM((tl * bsz, dm), BF16),
                        pltpu.VMEM((bsz, tl, width), F32)],
        compiler_params=_params("arbitrary"),
        name="s5_mixer",
    )(h, norm_w.reshape(1, dm), w_in, bmat, cmat, a_bar, d_skip.reshape(1, width))


def _s5_discretise_kernel(lre_ref, lim_ref, step_ref, bre_ref, bim_ref,
                          are_ref, aim_ref, bbre_ref, bbim_ref):
    lam_re, lam_im = lre_ref[...], lim_ref[...]
    dt = jnp.exp(step_ref[...])
    mag = jnp.exp(lam_re * dt)
    a_re = mag * jnp.cos(lam_im * dt)
    a_im = mag * jnp.sin(lam_im * dt)
    den = lam_re * lam_re + lam_im * lam_im
    f_re = ((a_re - 1.0) * lam_re + a_im * lam_im) / den
    f_im = (a_im * lam_re - (a_re - 1.0) * lam_im) / den
    are_ref[...] = a_re
    aim_ref[...] = a_im
    bbre_ref[...] = f_re * bre_ref[...] - f_im * bim_ref[...]
    bbim_ref[...] = f_re * bim_ref[...] + f_im * bre_ref[...]


def _s5_discretise(lam_re, lam_im, log_step, b_re, b_im, c_re, c_im):
    n_groups = lam_re.shape[0]
    n_tiles = n_groups // GROUPS_PER_LANE_TILE
    flat = (n_groups, STATE * GROUP)
    spread = lambda t: jnp.repeat(t, GROUP, axis=1)
    a_re, a_im, bb_re, bb_im = pl.pallas_call(
        _s5_discretise_kernel,
        out_shape=[jax.ShapeDtypeStruct(flat, F32)] * 4,
        name="s5_discretise",
    )(spread(lam_re), spread(lam_im), log_step.reshape(n_groups, 1),
      b_re.reshape(flat), b_im.reshape(flat))
    a_re, a_im = a_re[:, ::GROUP], a_im[:, ::GROUP]
    bb_re = bb_re.reshape(n_groups, STATE, GROUP)
    bb_im = bb_im.reshape(n_groups, STATE, GROUP)
    eye = jnp.eye(GROUPS_PER_LANE_TILE, dtype=F32)

    def in_map(bb):
        t = bb.reshape(n_tiles, GROUPS_PER_LANE_TILE, STATE, GROUP)
        t = jnp.einsum("kgpc,gh->kgchp", t, eye)
        return t.reshape(n_tiles, LANE, S5_HALF)

    def out_map(cc):
        t = cc.reshape(n_tiles, GROUPS_PER_LANE_TILE, GROUP, STATE)
        t = jnp.einsum("kgcp,gh->kgphc", t, eye)
        return t.reshape(n_tiles, S5_HALF, LANE)

    bmat = jnp.concatenate([in_map(bb_re), in_map(bb_im)], axis=2).astype(BF16)
    cmat = jnp.concatenate([out_map(c_re), -out_map(c_im)], axis=1).astype(BF16)
    a_bar = jnp.stack([a_re.reshape(n_tiles, S5_HALF), a_im.reshape(n_tiles, S5_HALF)], axis=1)
    return bmat, cmat, a_bar


def kernel(x, meta_tokens, norm_mix_pre, norm_mix_post, norm_ffn_pre, norm_ffn_post, ffn_w_gate,
           ffn_w_up, ffn_w_down, hyb_w_in, hyb_conv_w, dn_a_log, dn_dt_bias, dn_out_norm, hyb_w_out,
           ssm_w_in, ssm_lambda_re, ssm_lambda_im, ssm_log_step, ssm_b_re, ssm_b_im, ssm_c_re,
           ssm_c_im, ssm_d, ssm_w_glu):
    bsz, seq, d = x.shape
    l_real = N_META + seq
    lp = -(-l_real // SEQ_ALIGN) * SEQ_ALIGN
    w_in = hyb_w_in[0]
    n_gate = 2 * HA
    w_cat = jnp.concatenate([
        w_in[:, :4 * DA],
        jnp.pad(w_in[:, 4 * DA:4 * DA + n_gate], ((0, 0), (0, LANE - n_gate))),
        w_in[:, 4 * DA + n_gate:]], axis=1).astype(BF16)
    gate_par = jnp.zeros((2, LANE), F32)
    gate_par = gate_par.at[0, HA:2 * HA].set(dn_a_log[0].astype(F32))
    gate_par = gate_par.at[1, HA:2 * HA].set(dn_dt_bias[0].astype(F32))
    h, qkv_a, zs_a, gates, qkv_b = _hyb_in(x, meta_tokens.astype(x.dtype), lp, norm_mix_pre[0], w_cat,
                                           hyb_conv_w[0].astype(F32), gate_par)
    o_a = _deltanet(qkv_a, zs_a, gates, dn_out_norm[0].reshape(1, HEAD_DIM).astype(F32))
    o_b = _stickbreak(qkv_b)
    h = _mixer_out_ffn(_hyb_out_ffn_kernel, [o_a, o_b], hyb_w_out[0].astype(BF16), h,
                       norm_mix_post[0], norm_ffn_pre[0], ffn_w_gate[0].astype(BF16),
                       ffn_w_up[0].astype(BF16), ffn_w_down[0].astype(BF16), norm_ffn_post[0])

    h = h.reshape(bsz, lp, d)
    bmat, cmat, a_bar = _s5_discretise(ssm_lambda_re[0].astype(F32), ssm_lambda_im[0].astype(F32),
                                       ssm_log_step[0].astype(F32), ssm_b_re[0].astype(F32),
                                       ssm_b_im[0].astype(F32), ssm_c_re[0].astype(F32),
                                       ssm_c_im[0].astype(F32))
    y = _s5_mixer(h, norm_mix_pre[1], ssm_w_in[0].astype(BF16), bmat, cmat, a_bar,
                  ssm_d[0].astype(F32))
    return _mixer_out_ffn(_glu_out_ffn_kernel, [y], ssm_w_glu[0].astype(BF16),
                          h, norm_mix_post[1], norm_ffn_pre[1],
                          ffn_w_gate[1].astype(BF16), ffn_w_up[1].astype(BF16),
                          ffn_w_down[1].astype(BF16), norm_ffn_post[1], keep=(N_META, seq))
```
